```python
import math
import jax, jax.numpy as jnp
from jax import lax
import numpy as np

D_MODEL = 1024
BATCH = 8
SEQ = 4096
DEPTH = 1
DEC_BATCH = 8
DEC_SEQ = 2048
PAST_LEN = 128

SSM_EXPAND = 2
D_INNER = SSM_EXPAND * D_MODEL
SSM_HEAD_DIM = 64
SSM_HEADS = D_INNER // SSM_HEAD_DIM
SSM_GROUPS = 8
HEADS_PER_GROUP = SSM_HEADS // SSM_GROUPS
D_STATE = 128
CONV_K = 5
CHUNK = 128
CONV_CH = D_INNER + 2 * SSM_GROUPS * D_STATE
DT_MIN = 1e-3
DT_MAX = 1e-1
POOL_WIDTH = D_MODEL
POOL_WINDOWS = (2, 4, 8, 16)
POOL_GROUPS = len(POOL_WINDOWS)
POOL_GROUP_DIM = POOL_WIDTH // POOL_GROUPS
N_EXPERTS = 16
CAPACITY_FACTOR = 2
EXPERT_FF = 2816
EPS = 1e-6
IN_COLS = D_INNER + CONV_CH + 2 * SSM_HEADS + POOL_WIDTH + 2 * D_MODEL

kernel_name = "hybrid_ssd_pool_ec_encoder"


def rms_norm(x, w):
    xf = x.astype(jnp.float32)
    y = xf * lax.rsqrt(jnp.mean(xf * xf, axis=-1, keepdims=True) + EPS)
    return (y * w.astype(jnp.float32)).astype(x.dtype)


def grouped_rms_norm(y, w):
    b, l, d = y.shape
    yf = y.astype(jnp.float32).reshape(b, l, SSM_GROUPS, d // SSM_GROUPS)
    yf = yf * lax.rsqrt(jnp.mean(yf * yf, axis=-1, keepdims=True) + EPS)
    return (yf.reshape(b, l, d) * w.astype(jnp.float32)).astype(y.dtype)


def centred_dwconv(u, w, b):
    pad = CONV_K // 2
    y = lax.conv_general_dilated(
        u, w[:, None, :].astype(u.dtype), window_strides=(1,), padding=[(pad, pad)],
        dimension_numbers=("NWC", "WIO", "NWC"), feature_group_count=u.shape[-1])
    return y + b.astype(u.dtype)


def ssd_scan(xh, dt, a, bm, cm):
    bsz, seq, _, p = xh.shape
    nc = seq // CHUNK
    g, r, n = SSM_GROUPS, HEADS_PER_GROUP, D_STATE
    xdt = (xh.astype(jnp.float32) * dt[..., None]).reshape(bsz, nc, CHUNK, g, r, p)
    acum = jnp.cumsum((dt * a).reshape(bsz, nc, CHUNK, g, r), axis=2)
    bc = bm.astype(jnp.float32).reshape(bsz, nc, CHUNK, g, n)
    cc = cm.astype(jnp.float32).reshape(bsz, nc, CHUNK, g, n)
    seg = acum[:, :, :, None] - acum[:, :, None, :]
    lower = jnp.tril(jnp.ones((CHUNK, CHUNK), dtype=bool))[:, :, None, None]
    decay = jnp.exp(jnp.where(lower, seg, -jnp.inf))
    cb = jnp.einsum("bclgn,bcsgn->bclsg", cc, bc)
    y_diag = jnp.einsum("bclsgr,bcsgrp->bclgrp", cb[..., None] * decay, xdt)
    d_end = jnp.exp(acum[:, :, -1:] - acum)
    d_chunk = jnp.exp(acum[:, :, -1])
    d_start = jnp.exp(acum)

    def step(state, inp):
        x_c, b_c, c_c, de_c, dc_c, ds_c = inp
        y_off = jnp.einsum("blgn,bgrpn,blgr->blgrp", c_c, state, ds_c)
        new_state = state * dc_c[..., None, None] + jnp.einsum("blgn,blgr,blgrp->bgrpn", b_c, de_c, x_c)
        return new_state, y_off

    xs = (jnp.moveaxis(xdt, 1, 0), jnp.moveaxis(bc, 1, 0), jnp.moveaxis(cc, 1, 0),
          jnp.moveaxis(d_end, 1, 0), jnp.moveaxis(d_chunk, 1, 0), jnp.moveaxis(d_start, 1, 0))
    init = jnp.zeros((bsz, g, r, p, n), jnp.float32)
    _, y_off = lax.scan(step, init, xs)
    y = y_diag + jnp.moveaxis(y_off, 0, 1)
    return y.reshape(bsz, seq, SSM_HEADS, p)


def ssd_branch(z, xbc, dt_f_raw, dt_b_raw, conv_w, conv_b, dt_bias_f, dt_bias_b,
               a_log_f, a_log_b, d_skip, ssm_norm_w, w_ssm_out):
    bsz, seq, _ = z.shape
    xbc = jax.nn.silu(centred_dwconv(xbc, conv_w, conv_b))
    xs, bm, cm = jnp.split(xbc, [D_INNER, D_INNER + SSM_GROUPS * D_STATE], axis=-1)
    xh = xs.reshape(bsz, seq, SSM_HEADS, SSM_HEAD_DIM)
    bm = bm.reshape(bsz, seq, SSM_GROUPS, D_STATE)
    cm = cm.reshape(bsz, seq, SSM_GROUPS, D_STATE)
    dt_f = jax.nn.softplus(dt_f_raw.astype(jnp.float32) + dt_bias_f.astype(jnp.float32))
    dt_b = jax.nn.softplus(dt_b_raw.astype(jnp.float32) + dt_bias_b.astype(jnp.float32))
    a_f = -jnp.exp(a_log_f.astype(jnp.float32))
    a_b = -jnp.exp(a_log_b.astype(jnp.float32))
    y_f = ssd_scan(xh, dt_f, a_f, bm, cm)
    flip = lambda t: jnp.flip(t, axis=1)
    y_b = flip(ssd_scan(flip(xh), flip(dt_b), a_b, flip(bm), flip(cm)))
    y = y_f + y_b + xh.astype(jnp.float32) * d_skip.astype(jnp.float32)[:, None]
    y = y.reshape(bsz, seq, D_INNER).astype(z.dtype)
    y = grouped_rms_norm(y * jax.nn.silu(z), ssm_norm_w)
    return y @ w_ssm_out


def pool_branch(u, w_pool, pool_scale):
    bsz, seq, _ = u.shape
    uf = u.astype(jnp.float32).reshape(bsz, seq, POOL_GROUPS, POOL_GROUP_DIM)
    csum = jnp.concatenate([jnp.zeros((bsz, 1, POOL_GROUPS, POOL_GROUP_DIM), jnp.float32),
                            jnp.cumsum(uf, axis=1)], axis=1)
    pos = jnp.arange(seq)
    outs = []
    for gi, win in enumerate(POOL_WINDOWS):
        lo = win // 2
        hi = win - lo
        start = jnp.clip(pos - lo, 0, seq)
        end = jnp.clip(pos + hi, 0, seq)
        cg = csum[:, :, gi]
        cnt = (end - start).astype(jnp.float32)[None, :, None]
        outs.append((cg[:, end] - cg[:, start]) / cnt - uf[:, :, gi])
    pooled = jnp.stack(outs, axis=2).astype(u.dtype)
    mixed = jnp.einsum("blgc,gcd->blgd", pooled, w_pool)
    return mixed.reshape(bsz, seq, POOL_WIDTH) * pool_scale


def expert_choice_ffn(h, w_router, w_gate, w_up, w_down):
    bsz, seq, d = h.shape
    t = h.reshape(bsz * seq, d)
    n_tok = bsz * seq
    cap = CAPACITY_FACTOR * n_tok // N_EXPERTS
    probs = jax.nn.softmax(t.astype(jnp.float32) @ w_router.astype(jnp.float32), axis=-1)
    aff, idx = lax.top_k(probs.T, cap)
    xe = t[idx]
    hid = jax.nn.silu(jnp.einsum("ecd,edf->ecf", xe, w_gate)) * jnp.einsum("ecd,edf->ecf", xe, w_up)
    ye = jnp.einsum("ecf,efd->ecd", hid, w_down) * aff[..., None].astype(h.dtype)
    out = jnp.zeros_like(t).at[idx.reshape(-1)].add(ye.reshape(-1, d))
    return out.reshape(bsz, seq, d)


def encoder_layer(x, norm_mix_w, w_in, conv_w, conv_b, dt_bias_f, dt_bias_b, a_log_f, a_log_b,
                  d_skip, ssm_norm_w, w_ssm_out, w_pool, pool_scale, w_o, norm_ffn_w,
                  w_router, w_gate, w_up, w_down):
    h = rms_norm(x, norm_mix_w)
    proj = h @ w_in
    o1 = D_INNER
    o2 = o1 + CONV_CH
    o3 = o2 + SSM_HEADS
    o4 = o3 + SSM_HEADS
    o5 = o4 + POOL_WIDTH
    o6 = o5 + D_MODEL
    z, xbc, dt_f, dt_b, u_pool, g_a, g_b = jnp.split(proj, [o1, o2, o3, o4, o5, o6], axis=-1)
    y_a = ssd_branch(z, xbc, dt_f, dt_b, conv_w, conv_b, dt_bias_f, dt_bias_b,
                     a_log_f, a_log_b, d_skip, ssm_norm_w, w_ssm_out)
    y_b = pool_branch(u_pool, w_pool, pool_scale)
    merged = jax.nn.sigmoid(g_a) * y_a + jax.nn.sigmoid(g_b) * y_b
    x = x + merged @ w_o
    x = x + expert_choice_ffn(rms_norm(x, norm_ffn_w), w_router, w_gate, w_up, w_down)
    return x


def setup_inputs(seed: int = 0) -> dict:
    key = jax.random.key(seed)
    ks = jax.random.split(key, 24)
    f32 = jnp.float32

    def nrm(k, shape, scale):
        return jax.random.normal(k, shape, f32) * scale

    x_prompt = nrm(ks[0], (BATCH, SEQ, D_MODEL), 1.0)
    x_sample = nrm(ks[1], (DEC_BATCH, DEC_SEQ, D_MODEL), 1.0)
    norm_mix_w = 1.0 + nrm(ks[2], (DEPTH, D_MODEL), 0.02)
    w_in = nrm(ks[3], (DEPTH, D_MODEL, IN_COLS), D_MODEL ** -0.5)
    conv_w = nrm(ks[4], (DEPTH, CONV_K, CONV_CH), CONV_K ** -0.5)
    conv_b = nrm(ks[5], (DEPTH, CONV_CH), 0.02)
    dt_init = jnp.exp(jax.random.uniform(ks[6], (2, DEPTH, SSM_HEADS), f32,
                                         math.log(DT_MIN), math.log(DT_MAX)))
    dt_bias = dt_init + jnp.log(-jnp.expm1(-dt_init))
    a_log = jnp.log(jax.random.uniform(ks[7], (2, DEPTH, SSM_HEADS), f32, 1.0, 16.0))
    d_skip = 1.0 + nrm(ks[8], (DEPTH, SSM_HEADS), 0.1)
    ssm_norm_w = 1.0 + nrm(ks[9], (DEPTH, D_INNER), 0.02)
    w_ssm_out = nrm(ks[10], (DEPTH, D_INNER, D_MODEL), D_INNER ** -0.5)
    w_pool = nrm(ks[11], (DEPTH, POOL_GROUPS, POOL_GROUP_DIM, POOL_GROUP_DIM), POOL_GROUP_DIM ** -0.5)
    pool_scale = 1.0 + nrm(ks[12], (DEPTH, POOL_WIDTH), 0.02)
    w_o = nrm(ks[13], (DEPTH, D_MODEL, D_MODEL), D_MODEL ** -0.5)
    norm_ffn_w = 1.0 + nrm(ks[14], (DEPTH, D_MODEL), 0.02)
    w_router = nrm(ks[15], (DEPTH, D_MODEL, N_EXPERTS), D_MODEL ** -0.5)
    w_gate = nrm(ks[16], (DEPTH, N_EXPERTS, D_MODEL, EXPERT_FF), D_MODEL ** -0.5)
    w_up = nrm(ks[17], (DEPTH, N_EXPERTS, D_MODEL, EXPERT_FF), D_MODEL ** -0.5)
    w_down = nrm(ks[18], (DEPTH, N_EXPERTS, EXPERT_FF, D_MODEL), EXPERT_FF ** -0.5)
    norm_final_w = 1.0 + nrm(ks[19], (D_MODEL,), 0.02)
    return {
        "x_prompt": x_prompt, "x_sample": x_sample,
        "norm_mix_w": norm_mix_w, "w_in": w_in, "conv_w": conv_w, "conv_b": conv_b,
        "dt_bias_f": dt_bias[0], "dt_bias_b": dt_bias[1],
        "a_log_f": a_log[0], "a_log_b": a_log[1], "d_skip": d_skip,
        "ssm_norm_w": ssm_norm_w, "w_ssm_out": w_ssm_out,
        "w_pool": w_pool, "pool_scale": pool_scale, "w_o": w_o,
        "norm_ffn_w": norm_ffn_w, "w_router": w_router,
        "w_gate": w_gate, "w_up": w_up, "w_down": w_down,
        "norm_final_w": norm_final_w,
    }


def reference(x_prompt, x_sample, norm_mix_w, w_in, conv_w, conv_b, dt_bias_f, dt_bias_b,
              a_log_f, a_log_b, d_skip, ssm_norm_w, w_ssm_out, w_pool, pool_scale, w_o,
              norm_ffn_w, w_router, w_gate, w_up, w_down, norm_final_w):
    def trunk(x):
        for i in range(DEPTH):
            x = encoder_layer(x, norm_mix_w[i], w_in[i], conv_w[i], conv_b[i],
                              dt_bias_f[i], dt_bias_b[i], a_log_f[i], a_log_b[i], d_skip[i],
                              ssm_norm_w[i], w_ssm_out[i], w_pool[i], pool_scale[i], w_o[i],
                              norm_ffn_w[i], w_router[i], w_gate[i], w_up[i], w_down[i])
        return rms_norm(x, norm_final_w)

    y_prompt = trunk(x_prompt)
    y_sample = trunk(x_sample)
    return (y_prompt, y_sample)
```

```python
import functools

import jax
import jax.numpy as jnp
from jax import lax
from jax.experimental import pallas as pl
from jax.experimental.pallas import tpu as pltpu

F32 = jnp.float32
BF16 = jnp.bfloat16
I32 = jnp.int32

EPS = 1e-6
CHUNK = 128
LANES = 128
SUBLANES = 8
BF16_ROWS = 16
SSM_GROUPS = 8
POOL_WINDOWS = (2, 4, 8, 16)
CAPACITY_FACTOR = 2
VMEM_LIMIT = 56 * 1024 * 1024


def _cparams(sem):
    return pltpu.CompilerParams(dimension_semantics=sem, vmem_limit_bytes=VMEM_LIMIT)


def _iota(shape, dim):
    return lax.broadcasted_iota(I32, shape, dim)


def _sigmoid(x):
    return 1.0 / (1.0 + jnp.exp(-x))


def _split_bf16(x, terms):
    parts = []
    rem = x
    for _ in range(terms):
        p = rem.astype(BF16)
        parts.append(p)
        rem = rem - p.astype(F32)
    return parts


def _dot(a, b):
    return jnp.dot(a, b, preferred_element_type=F32)


def _dot_nt(a, b):
    return lax.dot_general(a, b, (((1,), (1,)), ((), ())), preferred_element_type=F32)


def _dot_exact_lhs(mask_bf, x, terms=3):
    out = None
    for p in _split_bf16(x, terms):
        d = _dot(mask_bf, p)
        out = d if out is None else out + d
    return out


def _in_proj_kernel(x_ref, nw_ref, w_ref, wdt_ref, main_ref, dt_ref, h_ref):
    @pl.when(pl.program_id(1) == 0)
    def _():
        x = x_ref[...]
        h = x * lax.rsqrt(jnp.mean(x * x, axis=-1, keepdims=True) + EPS) * nw_ref[...]
        hb, hl = _split_bf16(h, 2)
        h_ref[...] = hb
        wb, wl = _split_bf16(wdt_ref[...], 2)
        dt_ref[...] = _dot(hb, wb) + _dot(hl, wb) + _dot(hb, wl)

    main_ref[...] = _dot(h_ref[...], w_ref[...]).astype(main_ref.dtype)


def _in_proj(x2d, norm_w, w_main, w_dt, tm, tn):
    t, d = x2d.shape
    ncols = w_main.shape[1]
    return pl.pallas_call(
        _in_proj_kernel,
        grid=(t // tm, ncols // tn),
        in_specs=[
            pl.BlockSpec((tm, d), lambda i, j: (i, 0)),
            pl.BlockSpec((1, d), lambda i, j: (0, 0)),
            pl.BlockSpec((d, tn), lambda i, j: (0, j)),
            pl.BlockSpec((d, LANES), lambda i, j: (0, 0)),
        ],
        out_specs=[
            pl.BlockSpec((tm, tn), lambda i, j: (i, j)),
            pl.BlockSpec((tm, LANES), lambda i, j: (i, 0)),
        ],
        out_shape=[
            jax.ShapeDtypeStruct((t, ncols), BF16),
            jax.ShapeDtypeStruct((t, LANES), F32),
        ],
        scratch_shapes=[pltpu.VMEM((tm, d), BF16)],
        compiler_params=_cparams(("parallel", "arbitrary")),
        name="in_proj",
    )(x2d, norm_w, w_main, w_dt)


def _conv_kernel(cur_ref, prev_ref, next_ref, w_ref, b_ref, out_ref, buf_ref, *, tiles_per_seq):
    i = pl.program_id(0)
    tr = cur_ref.shape[0]
    halo = prev_ref.shape[0]
    pos = i % tiles_per_seq
    keep_prev = jnp.where(pos == 0, 0.0, 1.0)
    keep_next = jnp.where(pos == tiles_per_seq - 1, 0.0, 1.0)
    buf_ref[0:halo, :] = prev_ref[...].astype(F32) * keep_prev
    buf_ref[halo:halo + tr, :] = cur_ref[...].astype(F32)
    buf_ref[halo + tr:halo + tr + halo, :] = next_ref[...].astype(F32) * keep_next
    k = w_ref.shape[0]
    pad = k // 2
    acc = b_ref[...] + w_ref[0:1, :] * buf_ref[halo - pad:halo - pad + tr, :]
    for j in range(1, k):
        acc = acc + w_ref[j:j + 1, :] * buf_ref[halo - pad + j:halo - pad + j + tr, :]
    out_ref[...] = (acc * _sigmoid(acc)).astype(out_ref.dtype)


def _conv(main, conv_w, conv_b, seq, col0, tr, tc):
    t = main.shape[0]
    k, ch = conv_w.shape
    halo = BF16_ROWS
    cb0 = col0 // tc
    nhalo = t // halo
    rh = tr // halo
    return pl.pallas_call(
        functools.partial(_conv_kernel, tiles_per_seq=seq // tr),
        grid=(t // tr, ch // tc),
        in_specs=[
            pl.BlockSpec((tr, tc), lambda i, j: (i, cb0 + j)),
            pl.BlockSpec((halo, tc), lambda i, j: (jnp.maximum(i * rh - 1, 0), cb0 + j)),
            pl.BlockSpec((halo, tc), lambda i, j: (jnp.minimum((i + 1) * rh, nhalo - 1), cb0 + j)),
            pl.BlockSpec((k, tc), lambda i, j: (0, j)),
            pl.BlockSpec((1, tc), lambda i, j: (0, j)),
        ],
        out_specs=pl.BlockSpec((tr, tc), lambda i, j: (i, j)),
        out_shape=jax.ShapeDtypeStruct((t, ch), BF16),
        scratch_shapes=[pltpu.VMEM((tr + 2 * halo, tc), F32)],
        compiler_params=_cparams(("parallel", "parallel")),
        name="conv_silu",
    )(main, main, main, conv_w, conv_b)


def _expand_heads(v, h0, hpg, p):
    rows = v.shape[0]
    return jnp.concatenate(
        [jnp.broadcast_to(v[:, h0 + j:h0 + j + 1], (rows, p)) for j in range(hpg)], axis=1)


def _ssd_kernel(xs_ref, b_ref, c_ref, dt_ref, bias_ref, alog_ref, dskip_ref, y_ref, state_ref,
                *, reverse, heads, add_skip):
    @pl.when(pl.program_id(1) == 0)
    def _():
        state_ref[...] = jnp.zeros_like(state_ref)

    L = xs_ref.shape[0]
    d_inner = xs_ref.shape[1]
    p = d_inner // heads
    hpg = heads // SSM_GROUPS
    gw = hpg * p
    n = b_ref.shape[1] // SSM_GROUPS
    off = heads if reverse else 0

    v = dt_ref[...] + bias_ref[...]
    dt = jnp.maximum(v, 0.0) + jnp.log1p(jnp.exp(-jnp.abs(v)))
    da = dt * (-jnp.exp(alog_ref[...]))
    row = _iota((L, L), 0)
    col = _iota((L, L), 1)
    lower = col <= row
    upper = col >= row
    mask = upper if reverse else lower
    tri = jnp.where(mask, 1.0, 0.0).astype(BF16)
    acum = _dot_exact_lhs(tri, da)
    acum_t = acum.T
    last = 0 if reverse else L - 1
    a_last = acum[last:last + 1, :]
    d_start = jnp.exp(acum)
    d_end = jnp.exp(a_last - acum)
    d_chunk = jnp.exp(a_last)

    for g in range(SSM_GROUPS):
        h0 = off + g * hpg
        bg = b_ref[:, g * n:(g + 1) * n]
        cg = c_ref[:, g * n:(g + 1) * n]
        xg = xs_ref[:, g * gw:(g + 1) * gw].astype(F32)
        xdt = xg * _expand_heads(dt, h0, hpg, p)
        cb = _dot_nt(cg, bg)
        diag = []
        for j in range(hpg):
            h = h0 + j
            seg = acum[:, h:h + 1] - acum_t[h:h + 1, :]
            m = (cb * jnp.exp(jnp.where(mask, seg, -1e30))).astype(BF16)
            diag.append(_dot(m, xdt[:, j * p:(j + 1) * p].astype(BF16)))
        y = jnp.concatenate(diag, axis=1)
        st = state_ref[g]
        y = y + _dot(cg, st.astype(BF16)) * _expand_heads(d_start, h0, hpg, p)
        w = (xdt * _expand_heads(d_end, h0, hpg, p)).astype(BF16)
        bt = bg.astype(F32).T.astype(BF16)
        state_ref[g] = st * _expand_heads(d_chunk, h0, hpg, p) + _dot(bt, w)
        if add_skip:
            y = y + xg * dskip_ref[:, g * gw:(g + 1) * gw]
        y_ref[:, g * gw:(g + 1) * gw] = y.astype(y_ref.dtype)


def _ssd(xc, dt_raw, bias_all, alog_all, dskip_exp, batch, seq, d_inner, heads, reverse):
    t = xc.shape[0]
    nc = seq // CHUNK
    gn = (xc.shape[1] - d_inner) // 2
    bblk = d_inner // gn
    n = gn // SSM_GROUPS
    gw = d_inner // SSM_GROUPS

    def rowblk(b, c):
        return b * nc + (nc - 1 - c if reverse else c)

    return pl.pallas_call(
        functools.partial(_ssd_kernel, reverse=reverse, heads=heads, add_skip=not reverse),
        grid=(batch, nc),
        in_specs=[
            pl.BlockSpec((CHUNK, d_inner), lambda b, c: (rowblk(b, c), 0)),
            pl.BlockSpec((CHUNK, gn), lambda b, c: (rowblk(b, c), bblk)),
            pl.BlockSpec((CHUNK, gn), lambda b, c: (rowblk(b, c), bblk + 1)),
            pl.BlockSpec((CHUNK, LANES), lambda b, c: (rowblk(b, c), 0)),
            pl.BlockSpec((1, LANES), lambda b, c: (0, 0)),
            pl.BlockSpec((1, LANES), lambda b, c: (0, 0)),
            pl.BlockSpec((1, d_inner), lambda b, c: (0, 0)),
        ],
        out_specs=pl.BlockSpec((CHUNK, d_inner), lambda b, c: (rowblk(b, c), 0)),
        out_shape=jax.ShapeDtypeStruct((t, d_inner), BF16),
        scratch_shapes=[pltpu.VMEM((SSM_GROUPS, n, gw), F32)],
        compiler_params=_cparams(("parallel", "arbitrary")),
        name="ssd_bwd" if reverse else "ssd_fwd",
    )(xc, xc, xc, dt_raw, bias_all, alog_all, dskip_exp)


def _mix_kernel(yf_ref, yb_ref, z_ref, u_ref, up_ref, un_ref, ga_ref, gb_ref, x_ref,
                snw_ref, wso_ref, wp_ref, ps_ref, wo_ref, fnw_ref, wr_ref,
                x1_ref, h2_ref, pt_ref, *, tiles_per_seq, seq, n_experts):
    tm = x_ref.shape[0]
    d_inner = z_ref.shape[1]
    gw = d_inner // SSM_GROUPS
    halo = up_ref.shape[0]
    pos0 = (pl.program_id(0) % tiles_per_seq) * tm

    z = z_ref[...].astype(F32)
    y = (yf_ref[...].astype(F32) + yb_ref[...].astype(F32)) * (z * _sigmoid(z))
    parts = []
    for g in range(SSM_GROUPS):
        blk = y[:, g * gw:(g + 1) * gw]
        sc = lax.rsqrt(jnp.mean(blk * blk, axis=-1, keepdims=True) + EPS)
        parts.append((blk * sc * snw_ref[:, g * gw:(g + 1) * gw]).astype(BF16))
    y_a = _dot(jnp.concatenate(parts, axis=1), wso_ref[...])

    pw = u_ref.shape[1] // len(POOL_WINDOWS)
    keep_prev = jnp.where(pos0 > 0, 1.0, 0.0)
    keep_next = jnp.where(pos0 + tm < seq, 1.0, 0.0)
    tq = _iota((tm, tm), 0)
    sq = _iota((tm, tm), 1)
    tp = _iota((tm, halo), 0)
    jp = _iota((tm, halo), 1)
    tpos = pos0 + _iota((tm, 1), 0)
    pooled = []
    for gi, win in enumerate(POOL_WINDOWS):
        lo = win // 2
        hi = win - lo
        band_c = jnp.where((sq >= tq - lo) & (sq < tq + hi), 1.0, 0.0).astype(BF16)
        band_p = jnp.where(jp - halo >= tp - lo, keep_prev, 0.0).astype(BF16)
        band_n = jnp.where(jp + tm < tp + hi, keep_next, 0.0).astype(BF16)
        ug = u_ref[:, gi * pw:(gi + 1) * pw]
        s = (_dot(band_c, ug) + _dot(band_p, up_ref[:, gi * pw:(gi + 1) * pw])
             + _dot(band_n, un_ref[:, gi * pw:(gi + 1) * pw]))
        cnt = (jnp.minimum(tpos + hi, seq) - jnp.maximum(tpos - lo, 0)).astype(F32)
        pg = (s / cnt - ug.astype(F32)).astype(BF16)
        pooled.append(_dot(pg, wp_ref[gi]))
    y_b = jnp.concatenate(pooled, axis=1) * ps_ref[...]

    merged = _sigmoid(ga_ref[...].astype(F32)) * y_a + _sigmoid(gb_ref[...].astype(F32)) * y_b
    x1 = x_ref[...] + _dot(merged.astype(BF16), wo_ref[...])
    x1_ref[...] = x1

    h2 = x1 * lax.rsqrt(jnp.mean(x1 * x1, axis=-1, keepdims=True) + EPS) * fnw_ref[...]
    h2_ref[...] = h2
    hb, hl = _split_bf16(h2, 2)
    wb, wl = _split_bf16(wr_ref[...], 2)
    logits = _dot(hb, wb) + _dot(hl, wb) + _dot(hb, wl)
    logits = jnp.where(_iota(logits.shape, 1) < n_experts, logits, -1e30)
    e = jnp.exp(logits - jnp.max(logits, axis=-1, keepdims=True))
    probs = e / jnp.sum(e, axis=-1, keepdims=True)
    pt_ref[...] = probs.T[:n_experts, :]


def _mix(yf, yb, main, x2d, ssm_norm_w, w_ssm_out, w_pool, pool_scale, w_o, norm_ffn_w, w_router_pad,
         seq, d_inner, conv_ch, n_experts, tm):
    t, d = x2d.shape
    halo = BF16_ROWS
    rh = tm // halo
    nhalo = t // halo
    zb = 0
    ub = (d_inner + conv_ch) // d
    full = lambda shape: pl.BlockSpec(shape, lambda i: (0,) * len(shape))
    return pl.pallas_call(
        functools.partial(_mix_kernel, tiles_per_seq=seq // tm, seq=seq, n_experts=n_experts),
        grid=(t // tm,),
        in_specs=[
            pl.BlockSpec((tm, d_inner), lambda i: (i, 0)),
            pl.BlockSpec((tm, d_inner), lambda i: (i, 0)),
            pl.BlockSpec((tm, d_inner), lambda i: (i, zb)),
            pl.BlockSpec((tm, d), lambda i: (i, ub)),
            pl.BlockSpec((halo, d), lambda i: (jnp.maximum(i * rh - 1, 0), ub)),
            pl.BlockSpec((halo, d), lambda i: (jnp.minimum((i + 1) * rh, nhalo - 1), ub)),
            pl.BlockSpec((tm, d), lambda i: (i, ub + 1)),
            pl.BlockSpec((tm, d), lambda i: (i, ub + 2)),
            pl.BlockSpec((tm, d), lambda i: (i, 0)),
            full((1, d_inner)),
            full((d_inner, d)),
            full(w_pool.shape),
            full((1, d)),
            full((d, d)),
            full((1, d)),
            full((d, LANES)),
        ],
        out_specs=[
            pl.BlockSpec((tm, d), lambda i: (i, 0)),
            pl.BlockSpec((tm, d), lambda i: (i, 0)),
            pl.BlockSpec((n_experts, tm), lambda i: (0, i)),
        ],
        out_shape=[
            jax.ShapeDtypeStruct((t, d), F32),
            jax.ShapeDtypeStruct((t, d), F32),
            jax.ShapeDtypeStruct((n_experts, t), F32),
        ],
        compiler_params=_cparams(("parallel",)),
        name="mix",
    )(yf, yb, main, main, main, main, main, main, x2d,
      ssm_norm_w, w_ssm_out, w_pool, pool_scale, w_o, norm_ffn_w, w_router_pad)


def _route_kernel(p_ref, idx_ref, aff_ref, code_ref, off_ref, tot_ref, *, cap, ct):
    p = p_ref[0]
    r = p.shape[0]
    key = pltpu.bitcast(p, I32)

    def search(i, prefix):
        cand = prefix | lax.shift_left(jnp.int32(1), 30 - i)
        cnt = jnp.sum(jnp.where(key >= cand, 1, 0))
        return jnp.where(cnt >= cap, cand, prefix)

    tau = lax.fori_loop(0, 31, search, jnp.int32(0))

    lane_r = _iota((LANES, LANES), 0)
    lane_c = _iota((LANES, LANES), 1)
    tri_incl = jnp.where(lane_r <= lane_c, 1.0, 0.0).astype(BF16)
    rr = _iota((r, r), 0)
    rc = _iota((r, r), 1)
    tril_strict = jnp.where(rc < rr, 1.0, 0.0).astype(BF16)

    def cumsum(mask):
        mb = jnp.where(mask, 1.0, 0.0).astype(BF16)
        incl = _dot(mb, tri_incl)
        tot = jnp.broadcast_to(incl[:, LANES - 1:LANES], (r, LANES))
        off = _dot(tril_strict, tot.astype(BF16))
        return mb, incl, off, tot

    gt = key > tau
    eq = key == tau
    need = (cap - jnp.sum(jnp.where(gt, 1, 0))).astype(F32)
    mb_eq, incl_eq, off_eq, _ = cumsum(eq)
    rank_eq = off_eq + incl_eq - mb_eq.astype(F32)
    sel = gt | (eq & (rank_eq < need))
    mb, incl, off, tot = cumsum(sel)

    local = incl - mb.astype(F32)
    code_ref[0] = jnp.where(sel, local, -1.0).astype(I32)

    ones8 = jnp.ones((SUBLANES, LANES), BF16)
    tot_t = _dot_nt(ones8, mb)
    triu_strict = jnp.where(rr < rc, 1.0, 0.0).astype(BF16)
    off_t = _dot(tot_t.astype(BF16), triu_strict)
    off_ref[0] = off_t[0:1, :].astype(I32)
    tot_ref[0] = tot_t[0:1, :].astype(I32)

    incl_t = _dot_nt(jnp.where(lane_c <= lane_r, 1.0, 0.0).astype(BF16), mb).astype(BF16)
    p_t = _split_bf16(p.T, 3)
    incl_row = (off + tot)[:, 0:1]
    off_col = off[:, 0:1]
    for c0 in range(0, cap, ct):
        cvec = (c0 + _iota((1, ct), 1)).astype(F32)
        before = incl_row <= cvec
        rsel = jnp.sum(jnp.where(before, 1, 0), axis=0, keepdims=True)
        onehot = _iota((r, ct), 0) == rsel
        onehot_bf = jnp.where(onehot, 1.0, 0.0).astype(BF16)
        g_t = _dot(incl_t, onehot_bf)
        offc = jnp.sum(jnp.where(onehot, off_col, 0.0), axis=0, keepdims=True)
        target = cvec - offc
        lane = jnp.sum(jnp.where(g_t <= target, 1, 0), axis=0, keepdims=True)
        idx_ref[0, :, c0:c0 + ct] = rsel * LANES + lane
        a_t = _dot(p_t[0], onehot_bf) + _dot(p_t[1], onehot_bf) + _dot(p_t[2], onehot_bf)
        hit = _iota((LANES, ct), 0) == lane
        aff_ref[0, :, c0:c0 + ct] = jnp.sum(jnp.where(hit, a_t, 0.0), axis=0, keepdims=True)


def _route(probs_t, cap):
    e, t = probs_t.shape
    r = t // LANES
    ct = min(cap, 512)
    p3 = probs_t.reshape(e, r, LANES)
    return pl.pallas_call(
        functools.partial(_route_kernel, cap=cap, ct=ct),
        grid=(e,),
        in_specs=[pl.BlockSpec((1, r, LANES), lambda i: (i, 0, 0))],
        out_specs=[
            pl.BlockSpec((1, 1, cap), lambda i: (i, 0, 0)),
            pl.BlockSpec((1, 1, cap), lambda i: (i, 0, 0)),
            pl.BlockSpec((1, r, LANES), lambda i: (i, 0, 0)),
            pl.BlockSpec((1, 1, r), lambda i: (i, 0, 0)),
            pl.BlockSpec((1, 1, r), lambda i: (i, 0, 0)),
        ],
        out_shape=[
            jax.ShapeDtypeStruct((e, 1, cap), I32),
            jax.ShapeDtypeStruct((e, 1, cap), F32),
            jax.ShapeDtypeStruct((e, r, LANES), I32),
            jax.ShapeDtypeStruct((e, 1, r), I32),
            jax.ShapeDtypeStruct((e, 1, r), I32),
        ],
        compiler_params=_cparams(("parallel",)),
        name="route",
    )(p3)


def _gather_kernel(idx_ref, h_hbm, out_ref, sem):
    rows = out_ref.shape[0]

    def issue(i, carry):
        tok = idx_ref[0, 0, i]
        pltpu.make_async_copy(h_hbm.at[pl.ds(tok, 1), :], out_ref.at[pl.ds(i, 1), :], sem).start()
        return carry

    lax.fori_loop(0, rows, issue, 0)
    pltpu.make_async_copy(h_hbm.at[pl.ds(0, rows), :], out_ref, sem).wait()


def _gather(h2, idx_flat, tg):
    t, d = h2.shape
    nrows = idx_flat.shape[0]
    idx3 = idx_flat.reshape(nrows // tg, 1, tg)
    return pl.pallas_call(
        _gather_kernel,
        grid=(nrows // tg,),
        in_specs=[
            pl.BlockSpec((1, 1, tg), lambda i: (i, 0, 0), memory_space=pltpu.SMEM),
            pl.BlockSpec(memory_space=pl.ANY),
        ],
        out_specs=pl.BlockSpec((tg, d), lambda i: (i, 0)),
        out_shape=jax.ShapeDtypeStruct((nrows, d), h2.dtype),
        scratch_shapes=[pltpu.SemaphoreType.DMA],
        compiler_params=_cparams(("arbitrary",)),
        name="gather",
    )(idx3, h2)


def _ffn_kernel(xe_ref, aff_ref, wg_ref, wu_ref, wd_ref, out_ref, xb_ref, acc_ref):
    f = pl.program_id(2)

    @pl.when(f == 0)
    def _():
        xb_ref[...] = xe_ref[...].astype(BF16)

    xb = xb_ref[...]
    g = _dot(xb, wg_ref[...].astype(BF16))
    u = _dot(xb, wu_ref[...].astype(BF16))
    hid = (g * _sigmoid(g) * u).astype(BF16)
    contrib = _dot(hid, wd_ref[...].astype(BF16))

    @pl.when(f == 0)
    def _():
        acc_ref[...] = contrib

    @pl.when(f > 0)
    def _():
        acc_ref[...] += contrib

    @pl.when(f == pl.num_programs(2) - 1)
    def _():
        for j in range(aff_ref.shape[0]):
            a = jnp.broadcast_to(aff_ref[j:j + 1, :], (LANES, LANES)).T[:, 0:1]
            out_ref[j * LANES:(j + 1) * LANES, :] = acc_ref[j * LANES:(j + 1) * LANES, :] * a


def _ffn(xe, aff2d, w_gate, w_up, w_down, cap, rs, tf):
    nrows, d = xe.shape
    e, _, ff = w_gate.shape
    rpe = cap // rs
    return pl.pallas_call(
        _ffn_kernel,
        grid=(e, rpe, ff // tf),
        in_specs=[
            pl.BlockSpec((rs, d), lambda ei, r, f: (ei * rpe + r, 0)),
            pl.BlockSpec((rs // LANES, LANES), lambda ei, r, f: (ei * rpe + r, 0)),
            pl.BlockSpec((None, d, tf), lambda ei, r, f: (ei, 0, f)),
            pl.BlockSpec((None, d, tf), lambda ei, r, f: (ei, 0, f)),
            pl.BlockSpec((None, tf, d), lambda ei, r, f: (ei, f, 0)),
        ],
        out_specs=pl.BlockSpec((rs, d), lambda ei, r, f: (ei * rpe + r, 0)),
        out_shape=jax.ShapeDtypeStruct((nrows, d), F32),
        scratch_shapes=[pltpu.VMEM((rs, d), BF16), pltpu.VMEM((rs, d), F32)],
        compiler_params=_cparams(("parallel", "parallel", "arbitrary")),
        name="expert_ffn",
    )(xe, aff2d, w_gate, w_up, w_down)


def _combine_kernel(off_ref, tot_ref, code_ref, x1_ref, fw_ref, ye_hbm, out_ref, buf_ref, sem,
                    *, cap, n_experts, qc, final_norm):
    blk = pl.program_id(0)
    tb = x1_ref.shape[0]

    @pl.when(blk == 0)
    def _():
        buf_ref[...] = jnp.zeros_like(buf_ref)

    qbase = jnp.int32(0)
    shifts = []
    for e in range(n_experts):
        s = off_ref[e, blk]
        n = tot_ref[e, blk]
        c_lo = s // SUBLANES
        nch = jnp.where(n > 0, (s + n - 1) // SUBLANES - c_lo + 1, 0)

        def issue(i, carry, e=e, c_lo=c_lo, qbase=qbase):
            src = pl.multiple_of(e * cap + (c_lo + i) * SUBLANES, SUBLANES)
            dst = pl.multiple_of(qbase + i * SUBLANES, SUBLANES)
            pltpu.make_async_copy(ye_hbm.at[pl.ds(src, SUBLANES), :],
                                  buf_ref.at[pl.ds(dst, SUBLANES), :], sem).start()
            return carry

        lax.fori_loop(0, nch, issue, 0)
        shifts.append(qbase + (s - c_lo * SUBLANES))
        qbase = qbase + nch * SUBLANES

    def drain(i, carry):
        pltpu.make_async_copy(ye_hbm.at[pl.ds(0, SUBLANES), :],
                              buf_ref.at[pl.ds(0, SUBLANES), :], sem).wait()
        return carry

    lax.fori_loop(0, qbase // SUBLANES, drain, 0)

    code_t = code_ref[:, 0, :].astype(F32).T
    nq = (qbase + qc - 1) // qc

    def seg_sum(k, acc):
        q0 = pl.multiple_of(k * qc, qc)
        qi = q0 + _iota((tb, qc), 1)
        s_mat = jnp.zeros((tb, qc), F32)
        for e in range(n_experts):
            ce = code_t[:, e:e + 1].astype(I32)
            tgt = jnp.where(ce >= 0, ce + shifts[e], -1)
            s_mat = s_mat + jnp.where(tgt == qi, 1.0, 0.0)
        rows = buf_ref[pl.ds(q0, qc), :]
        hi, lo = _split_bf16(rows, 2)
        sb = s_mat.astype(BF16)
        return acc + _dot(sb, hi) + _dot(sb, lo)

    ffn = lax.fori_loop(0, nq, seg_sum, jnp.zeros(x1_ref.shape, F32))
    x2 = x1_ref[...] + ffn
    if final_norm:
        x2 = x2 * lax.rsqrt(jnp.mean(x2 * x2, axis=-1, keepdims=True) + EPS) * fw_ref[...]
    out_ref[...] = x2


def _combine(off, tot, code, x1, ye, final_w, cap, final_norm):
    t, d = x1.shape
    e = code.shape[0]
    tb = LANES
    qc = 256
    max_rows = e * (tb + 2 * SUBLANES)
    buf_rows = ((max_rows + qc - 1) // qc) * qc
    return pl.pallas_call(
        functools.partial(_combine_kernel, cap=cap, n_experts=e, qc=qc, final_norm=final_norm),
        grid_spec=pltpu.PrefetchScalarGridSpec(
            num_scalar_prefetch=2,
            grid=(t // tb,),
            in_specs=[
                pl.BlockSpec((e, None, 1, tb), lambda i, *_: (0, i, 0, 0)),
                pl.BlockSpec((tb, d), lambda i, *_: (i, 0)),
                pl.BlockSpec((1, d), lambda i, *_: (0, 0)),
                pl.BlockSpec(memory_space=pl.ANY),
            ],
            out_specs=pl.BlockSpec((tb, d), lambda i, *_: (i, 0)),
            scratch_shapes=[pltpu.VMEM((buf_rows, d), F32), pltpu.SemaphoreType.DMA],
        ),
        out_shape=jax.ShapeDtypeStruct((t, d), F32),
        compiler_params=_cparams(("arbitrary",)),
        name="combine",
    )(off, tot, code.reshape(e, t // tb, 1, tb), x1, final_w, ye)


def _pick(n, prefs):
    for p in prefs:
        if n % p == 0:
            return p
    raise ValueError(f"no tile size for {n}")


def _layer(x, lw, final_w, final_norm):
    batch, seq, d = x.shape
    t = batch * seq
    heads = lw["dt_bias_f"].shape[-1]
    d_inner = lw["ssm_norm_w"].shape[-1]
    conv_ch = lw["conv_w"].shape[-1]
    n_experts = lw["w_router"].shape[-1]
    cap = CAPACITY_FACTOR * t // n_experts
    assert seq % CHUNK == 0 and cap % LANES == 0 and 2 * heads <= LANES

    x2d = x.reshape(t, d)
    main, dt_raw = _in_proj(x2d, lw["norm_mix_w"], lw["w_main"], lw["w_dt"],
                            _pick(t, (512, 256, 128)), _pick(lw["w_main"].shape[1], (1024, 512)))
    xc = _conv(main, lw["conv_w"], lw["conv_b"], seq, d_inner,
               _pick(seq, (512, 256, 128)), _pick(conv_ch, (1024, 512)))
    yf = _ssd(xc, dt_raw, lw["dt_bias_all"], lw["a_log_all"], lw["d_skip_exp"],
              batch, seq, d_inner, heads, reverse=False)
    yb = _ssd(xc, dt_raw, lw["dt_bias_all"], lw["a_log_all"], lw["d_skip_exp"],
              batch, seq, d_inner, heads, reverse=True)
    x1, h2, probs_t = _mix(yf, yb, main, x2d, lw["ssm_norm_w"], lw["w_ssm_out"], lw["w_pool"],
                           lw["pool_scale"], lw["w_o"], lw["norm_ffn_w"], lw["w_router_pad"],
                           seq, d_inner, conv_ch, n_experts, _pick(seq, (256, 128)))
    idx, aff, code, off, tot = _route(probs_t, cap)
    xe = _gather(h2, idx.reshape(n_experts * cap), _pick(cap, (512, 256, 128)))
    ye = _ffn(xe, aff.reshape(n_experts * cap // LANES, LANES), lw["w_gate"], lw["w_up"], lw["w_down"],
              cap, _pick(cap, (1024, 512, 256, 128)), _pick(lw["w_gate"].shape[-1], (256, 128)))
    r = t // LANES
    out = _combine(off.reshape(n_experts, r), tot.reshape(n_experts, r), code, x1, ye, final_w,
                   cap, final_norm)
    return out.reshape(batch, seq, d)


def _prep_layer(i, norm_mix_w, w_in, conv_w, conv_b, dt_bias_f, dt_bias_b, a_log_f, a_log_b, d_skip,
                ssm_norm_w, w_ssm_out, w_pool, pool_scale, w_o, norm_ffn_w, w_router, w_gate, w_up, w_down):
    d = w_in.shape[1]
    heads = dt_bias_f.shape[-1]
    d_inner = ssm_norm_w.shape[-1]
    conv_ch = conv_w.shape[-1]
    p = d_inner // heads
    o2 = d_inner + conv_ch
    o4 = o2 + 2 * heads
    wi = w_in[i]
    w_main = jnp.concatenate([wi[:, :o2], wi[:, o4:]], axis=1).astype(BF16)
    w_dt = jnp.pad(wi[:, o2:o4], ((0, 0), (0, LANES - 2 * heads)))
    row = lambda v: v.reshape(1, -1).astype(F32)
    padl = lambda v: jnp.pad(v.reshape(1, -1).astype(F32), ((0, 0), (0, LANES - v.size)))
    return {
        "norm_mix_w": row(norm_mix_w[i]), "w_main": w_main, "w_dt": w_dt,
        "conv_w": conv_w[i], "conv_b": row(conv_b[i]),
        "dt_bias_f": dt_bias_f[i],
        "dt_bias_all": padl(jnp.concatenate([dt_bias_f[i], dt_bias_b[i]])),
        "a_log_all": padl(jnp.concatenate([a_log_f[i], a_log_b[i]])),
        "d_skip_exp": row(jnp.repeat(d_skip[i], p)),
        "ssm_norm_w": row(ssm_norm_w[i]), "w_ssm_out": w_ssm_out[i].astype(BF16),
        "w_pool": w_pool[i].astype(BF16), "pool_scale": row(pool_scale[i]),
        "w_o": w_o[i].astype(BF16), "norm_ffn_w": row(norm_ffn_w[i]),
        "w_router": w_router[i],
        "w_router_pad": jnp.pad(w_router[i].astype(F32), ((0, 0), (0, LANES - w_router.shape[-1]))),
        "w_gate": w_gate[i], "w_up": w_up[i], "w_down": w_down[i],
    }


def kernel(x_prompt, x_sample, norm_mix_w, w_in, conv_w, conv_b, dt_bias_f, dt_bias_b, a_log_f, a_log_b,
           d_skip, ssm_norm_w, w_ssm_out, w_pool, pool_scale, w_o, norm_ffn_w, w_router, w_gate, w_up,
           w_down, norm_final_w):
    depth = w_in.shape[0]
    layers = [_prep_layer(i, norm_mix_w, w_in, conv_w, conv_b, dt_bias_f, dt_bias_b, a_log_f, a_log_b,
                          d_skip, ssm_norm_w, w_ssm_out, w_pool, pool_scale, w_o, norm_ffn_w, w_router,
                          w_gate, w_up, w_down) for i in range(depth)]
    final_w = norm_final_w.reshape(1, -1).astype(F32)

    def trunk(x):
        for i, lw in enumerate(layers):
            x = _layer(x, lw, final_w, final_norm=(i == depth - 1))
        return x

    return (trunk(x_prompt), trunk(x_sample))
```

```python
import functools

import jax
import jax.numpy as jnp
from jax import lax
from jax.experimental import pallas as pl
from jax.experimental.pallas import tpu as pltpu

F32 = jnp.float32
BF16 = jnp.bfloat16
I32 = jnp.int32

EPS = 1e-6
CHUNK = 128
LANES = 128
SUBLANES = 8
BF16_ROWS = 16
SSM_GROUPS = 8
POOL_WINDOWS = (2, 4, 8, 16)
CAPACITY_FACTOR = 2
VMEM_LIMIT = 56 * 1024 * 1024


def _cparams(sem):
    return pltpu.CompilerParams(dimension_semantics=sem, vmem_limit_bytes=VMEM_LIMIT)


def _iota(shape, dim):
    return lax.broadcasted_iota(I32, shape, dim)


def _sigmoid(x):
    return 1.0 / (1.0 + jnp.exp(-x))


def _split_bf16(x, terms):
    parts = []
    rem = x
    for _ in range(terms):
        p = rem.astype(BF16)
        parts.append(p)
        rem = rem - p.astype(F32)
    return parts


def _dot(a, b):
    return jnp.dot(a, b, preferred_element_type=F32)


def _dot_nt(a, b):
    return lax.dot_general(a, b, (((1,), (1,)), ((), ())), preferred_element_type=F32)


def _dot_exact_lhs(mask_bf, x, terms=3):
    out = None
    for p in _split_bf16(x, terms):
        d = _dot(mask_bf, p)
        out = d if out is None else out + d
    return out


def _in_proj_kernel(x_ref, nw_ref, w_ref, wdt_ref, main_ref, dt_ref, h_ref):
    @pl.when(pl.program_id(1) == 0)
    def _():
        x = x_ref[...]
        h = x * lax.rsqrt(jnp.mean(x * x, axis=-1, keepdims=True) + EPS) * nw_ref[...]
        hb, hl = _split_bf16(h, 2)
        h_ref[...] = hb
        wb, wl = _split_bf16(wdt_ref[...], 2)
        dt_ref[...] = _dot(hb, wb) + _dot(hl, wb) + _dot(hb, wl)

    main_ref[...] = _dot(h_ref[...], w_ref[...]).astype(main_ref.dtype)


def _in_proj(x2d, norm_w, w_main, w_dt, tm, tn):
    t, d = x2d.shape
    ncols = w_main.shape[1]
    return pl.pallas_call(
        _in_proj_kernel,
        grid=(t // tm, ncols // tn),
        in_specs=[
            pl.BlockSpec((tm, d), lambda i, j: (i, 0)),
            pl.BlockSpec((1, d), lambda i, j: (0, 0)),
            pl.BlockSpec((d, tn), lambda i, j: (0, j)),
            pl.BlockSpec((d, LANES), lambda i, j: (0, 0)),
        ],
        out_specs=[
            pl.BlockSpec((tm, tn), lambda i, j: (i, j)),
            pl.BlockSpec((tm, LANES), lambda i, j: (i, 0)),
        ],
        out_shape=[
            jax.ShapeDtypeStruct((t, ncols), BF16),
            jax.ShapeDtypeStruct((t, LANES), F32),
        ],
        scratch_shapes=[pltpu.VMEM((tm, d), BF16)],
        compiler_params=_cparams(("parallel", "arbitrary")),
        name="in_proj",
    )(x2d, norm_w, w_main, w_dt)


def _conv_kernel(cur_ref, prev_ref, next_ref, w_ref, b_ref, out_ref, buf_ref, *, tiles_per_seq):
    i = pl.program_id(0)
    tr = cur_ref.shape[0]
    halo = prev_ref.shape[0]
    pos = i % tiles_per_seq
    keep_prev = jnp.where(pos == 0, 0.0, 1.0)
    keep_next = jnp.where(pos == tiles_per_seq - 1, 0.0, 1.0)
    buf_ref[0:halo, :] = prev_ref[...].astype(F32) * keep_prev
    buf_ref[halo:halo + tr, :] = cur_ref[...].astype(F32)
    buf_ref[halo + tr:halo + tr + halo, :] = next_ref[...].astype(F32) * keep_next
    k = w_ref.shape[0]
    pad = k // 2
    acc = b_ref[...] + w_ref[0:1, :] * buf_ref[halo - pad:halo - pad + tr, :]
    for j in range(1, k):
        acc = acc + w_ref[j:j + 1, :] * buf_ref[halo - pad + j:halo - pad + j + tr, :]
    out_ref[...] = (acc * _sigmoid(acc)).astype(out_ref.dtype)


def _conv(main, conv_w, conv_b, seq, col0, tr, tc):
    t = main.shape[0]
    k, ch = conv_w.shape
    halo = BF16_ROWS
    cb0 = col0 // tc
    nhalo = t // halo
    rh = tr // halo
    return pl.pallas_call(
        functools.partial(_conv_kernel, tiles_per_seq=seq // tr),
        grid=(t // tr, ch // tc),
        in_specs=[
            pl.BlockSpec((tr, tc), lambda i, j: (i, cb0 + j)),
            pl.BlockSpec((halo, tc), lambda i, j: (jnp.maximum(i * rh - 1, 0), cb0 + j)),
            pl.BlockSpec((halo, tc), lambda i, j: (jnp.minimum((i + 1) * rh, nhalo - 1), cb0 + j)),
            pl.BlockSpec((k, tc), lambda i, j: (0, j)),
            pl.BlockSpec((1, tc), lambda i, j: (0, j)),
        ],
        out_specs=pl.BlockSpec((tr, tc), lambda i, j: (i, j)),
        out_shape=jax.ShapeDtypeStruct((t, ch), BF16),
        scratch_shapes=[pltpu.VMEM((tr + 2 * halo, tc), F32)],
        compiler_params=_cparams(("parallel", "parallel")),
        name="conv_silu",
    )(main, main, main, conv_w, conv_b)


def _ssd_kernel(xs_ref, b_ref, c_ref, dt_ref, bias_ref, alog_ref, dskip_ref, y_ref, state_ref,
                *, reverse, heads, add_skip):
    @pl.when(pl.program_id(1) == 0)
    def _():
        state_ref[...] = jnp.zeros_like(state_ref)

    L = xs_ref.shape[0]
    d_inner = xs_ref.shape[1]
    p = d_inner // heads
    hpg = heads // SSM_GROUPS
    n = b_ref.shape[1] // SSM_GROUPS
    assert n == L
    off = heads if reverse else 0

    v = dt_ref[...] + bias_ref[...]
    dt = jnp.maximum(v, 0.0) + jnp.log1p(jnp.exp(-jnp.abs(v)))
    da = dt * (-jnp.exp(alog_ref[...]))
    row = _iota((L, L), 0)
    col = _iota((L, L), 1)
    lower = col <= row
    upper = col >= row
    mask = upper if reverse else lower
    tri = jnp.where(mask, 1.0, 0.0).astype(BF16)
    acum = _dot_exact_lhs(tri, da)
    acum_t = acum.T
    ldt_t = jnp.log(dt).T
    last = 0 if reverse else L - 1
    a_last_t = acum_t[:, last:last + 1]
    src_t = acum_t - ldt_t
    wrow_t = jnp.exp(a_last_t - src_t)
    dchunk_t = jnp.exp(a_last_t)

    for g in range(SSM_GROUPS):
        bg = b_ref[:, g * n:(g + 1) * n]
        cg = c_ref[:, g * n:(g + 1) * n]
        cgf = cg.astype(F32)
        cb = _dot_nt(cg, bg)
        btf = bg.astype(F32).T
        for j in range(hpg):
            hh = g * hpg + j
            h = off + hh
            xh = xs_ref[:, hh * p:(hh + 1) * p]
            st = state_ref[hh]
            colb = jnp.broadcast_to(acum[:, h:h + 1], (L, L))
            m = (cb * jnp.exp(jnp.where(mask, colb - src_t[h:h + 1, :], -1e30))).astype(BF16)
            cs = (cgf * jnp.exp(colb)).astype(BF16)
            y = _dot(jnp.concatenate([m, cs], axis=1),
                     jnp.concatenate([xh, st.astype(BF16)], axis=0))
            bts = (btf * wrow_t[h:h + 1, :]).astype(BF16)
            state_ref[hh] = st * dchunk_t[h:h + 1, :] + _dot(bts, xh)
            if add_skip:
                y = y + xh.astype(F32) * dskip_ref[:, hh * p:(hh + 1) * p]
            y_ref[:, hh * p:(hh + 1) * p] = y.astype(y_ref.dtype)


def _ssd(xc, dt_raw, bias_all, alog_all, dskip_exp, batch, seq, d_inner, heads, reverse):
    t = xc.shape[0]
    nc = seq // CHUNK
    gn = (xc.shape[1] - d_inner) // 2
    bblk = d_inner // gn
    n = gn // SSM_GROUPS

    def rowblk(b, c):
        return b * nc + (nc - 1 - c if reverse else c)

    return pl.pallas_call(
        functools.partial(_ssd_kernel, reverse=reverse, heads=heads, add_skip=not reverse),
        grid=(batch, nc),
        in_specs=[
            pl.BlockSpec((CHUNK, d_inner), lambda b, c: (rowblk(b, c), 0)),
            pl.BlockSpec((CHUNK, gn), lambda b, c: (rowblk(b, c), bblk)),
            pl.BlockSpec((CHUNK, gn), lambda b, c: (rowblk(b, c), bblk + 1)),
            pl.BlockSpec((CHUNK, LANES), lambda b, c: (rowblk(b, c), 0)),
            pl.BlockSpec((1, LANES), lambda b, c: (0, 0)),
            pl.BlockSpec((1, LANES), lambda b, c: (0, 0)),
            pl.BlockSpec((1, d_inner), lambda b, c: (0, 0)),
        ],
        out_specs=pl.BlockSpec((CHUNK, d_inner), lambda b, c: (rowblk(b, c), 0)),
        out_shape=jax.ShapeDtypeStruct((t, d_inner), BF16),
        scratch_shapes=[pltpu.VMEM((heads, n, d_inner // heads), F32)],
        compiler_params=_cparams(("parallel", "arbitrary")),
        name="ssd_bwd" if reverse else "ssd_fwd",
    )(xc, xc, xc, dt_raw, bias_all, alog_all, dskip_exp)


def _mix_kernel(yf_ref, yb_ref, z_ref, u_ref, up_ref, un_ref, ga_ref, gb_ref, x_ref,
                snw_ref, wso_ref, wp_ref, ps_ref, wo_ref, fnw_ref, wr_ref,
                x1_ref, h2_ref, pt_ref, *, tiles_per_seq, seq, n_experts):
    tm = x_ref.shape[0]
    d_inner = z_ref.shape[1]
    gw = d_inner // SSM_GROUPS
    halo = up_ref.shape[0]
    pos0 = (pl.program_id(0) % tiles_per_seq) * tm

    z = z_ref[...].astype(F32)
    y = (yf_ref[...].astype(F32) + yb_ref[...].astype(F32)) * (z * _sigmoid(z))
    parts = []
    for g in range(SSM_GROUPS):
        blk = y[:, g * gw:(g + 1) * gw]
        sc = lax.rsqrt(jnp.mean(blk * blk, axis=-1, keepdims=True) + EPS)
        parts.append((blk * sc * snw_ref[:, g * gw:(g + 1) * gw]).astype(BF16))
    y_a = _dot(jnp.concatenate(parts, axis=1), wso_ref[...])

    pw = u_ref.shape[1] // len(POOL_WINDOWS)
    keep_prev = jnp.where(pos0 > 0, 1.0, 0.0)
    keep_next = jnp.where(pos0 + tm < seq, 1.0, 0.0)
    tq = _iota((tm, tm), 0)
    sq = _iota((tm, tm), 1)
    tp = _iota((tm, halo), 0)
    jp = _iota((tm, halo), 1)
    tpos = pos0 + _iota((tm, 1), 0)
    pooled = []
    for gi, win in enumerate(POOL_WINDOWS):
        lo = win // 2
        hi = win - lo
        band_c = jnp.where((sq >= tq - lo) & (sq < tq + hi), 1.0, 0.0).astype(BF16)
        band_p = jnp.where(jp - halo >= tp - lo, keep_prev, 0.0).astype(BF16)
        band_n = jnp.where(jp + tm < tp + hi, keep_next, 0.0).astype(BF16)
        ug = u_ref[:, gi * pw:(gi + 1) * pw]
        s = (_dot(band_c, ug) + _dot(band_p, up_ref[:, gi * pw:(gi + 1) * pw])
             + _dot(band_n, un_ref[:, gi * pw:(gi + 1) * pw]))
        cnt = (jnp.minimum(tpos + hi, seq) - jnp.maximum(tpos - lo, 0)).astype(F32)
        pg = (s / cnt - ug.astype(F32)).astype(BF16)
        pooled.append(_dot(pg, wp_ref[gi]))
    y_b = jnp.concatenate(pooled, axis=1) * ps_ref[...]

    merged = _sigmoid(ga_ref[...].astype(F32)) * y_a + _sigmoid(gb_ref[...].astype(F32)) * y_b
    x1 = x_ref[...] + _dot(merged.astype(BF16), wo_ref[...])
    x1_ref[...] = x1

    h2 = x1 * lax.rsqrt(jnp.mean(x1 * x1, axis=-1, keepdims=True) + EPS) * fnw_ref[...]
    h2_ref[...] = h2
    hb, hl = _split_bf16(h2, 2)
    wb, wl = _split_bf16(wr_ref[...], 2)
    logits = _dot(hb, wb) + _dot(hl, wb) + _dot(hb, wl)
    logits = jnp.where(_iota(logits.shape, 1) < n_experts, logits, -1e30)
    e = jnp.exp(logits - jnp.max(logits, axis=-1, keepdims=True))
    probs = e / jnp.sum(e, axis=-1, keepdims=True)
    pt_ref[...] = probs.T[:n_experts, :]


def _mix(yf, yb, main, x2d, ssm_norm_w, w_ssm_out, w_pool, pool_scale, w_o, norm_ffn_w, w_router_pad,
         seq, d_inner, conv_ch, n_experts, tm):
    t, d = x2d.shape
    halo = BF16_ROWS
    rh = tm // halo
    nhalo = t // halo
    zb = 0
    ub = (d_inner + conv_ch) // d
    full = lambda shape: pl.BlockSpec(shape, lambda i: (0,) * len(shape))
    return pl.pallas_call(
        functools.partial(_mix_kernel, tiles_per_seq=seq // tm, seq=seq, n_experts=n_experts),
        grid=(t // tm,),
        in_specs=[
            pl.BlockSpec((tm, d_inner), lambda i: (i, 0)),
            pl.BlockSpec((tm, d_inner), lambda i: (i, 0)),
            pl.BlockSpec((tm, d_inner), lambda i: (i, zb)),
            pl.BlockSpec((tm, d), lambda i: (i, ub)),
            pl.BlockSpec((halo, d), lambda i: (jnp.maximum(i * rh - 1, 0), ub)),
            pl.BlockSpec((halo, d), lambda i: (jnp.minimum((i + 1) * rh, nhalo - 1), ub)),
            pl.BlockSpec((tm, d), lambda i: (i, ub + 1)),
            pl.BlockSpec((tm, d), lambda i: (i, ub + 2)),
            pl.BlockSpec((tm, d), lambda i: (i, 0)),
            full((1, d_inner)),
            full((d_inner, d)),
            full(w_pool.shape),
            full((1, d)),
            full((d, d)),
            full((1, d)),
            full((d, LANES)),
        ],
        out_specs=[
            pl.BlockSpec((tm, d), lambda i: (i, 0)),
            pl.BlockSpec((tm, d), lambda i: (i, 0)),
            pl.BlockSpec((n_experts, tm), lambda i: (0, i)),
        ],
        out_shape=[
            jax.ShapeDtypeStruct((t, d), F32),
            jax.ShapeDtypeStruct((t, d), F32),
            jax.ShapeDtypeStruct((n_experts, t), F32),
        ],
        compiler_params=_cparams(("parallel",)),
        name="mix",
    )(yf, yb, main, main, main, main, main, main, x2d,
      ssm_norm_w, w_ssm_out, w_pool, pool_scale, w_o, norm_ffn_w, w_router_pad)


def _route_kernel(p_ref, idx_ref, aff_ref, code_ref, off_ref, tot_ref, *, cap, ct):
    p = p_ref[0]
    r = p.shape[0]
    key = pltpu.bitcast(p, I32)

    def search(i, prefix):
        cand = prefix | lax.shift_left(jnp.int32(1), 30 - i)
        cnt = jnp.sum(jnp.where(key >= cand, 1, 0))
        return jnp.where(cnt >= cap, cand, prefix)

    tau = lax.fori_loop(0, 31, search, jnp.int32(0))

    lane_r = _iota((LANES, LANES), 0)
    lane_c = _iota((LANES, LANES), 1)
    tri_incl = jnp.where(lane_r <= lane_c, 1.0, 0.0).astype(BF16)
    rr = _iota((r, r), 0)
    rc = _iota((r, r), 1)
    tril_strict = jnp.where(rc < rr, 1.0, 0.0).astype(BF16)

    def cumsum(mask):
        mb = jnp.where(mask, 1.0, 0.0).astype(BF16)
        incl = _dot(mb, tri_incl)
        tot = jnp.broadcast_to(incl[:, LANES - 1:LANES], (r, LANES))
        off = _dot(tril_strict, tot.astype(BF16))
        return mb, incl, off, tot

    gt = key > tau
    eq = key == tau
    need = (cap - jnp.sum(jnp.where(gt, 1, 0))).astype(F32)
    mb_eq, incl_eq, off_eq, _ = cumsum(eq)
    rank_eq = off_eq + incl_eq - mb_eq.astype(F32)
    sel = gt | (eq & (rank_eq < need))
    mb, incl, off, tot = cumsum(sel)

    local = incl - mb.astype(F32)
    code_ref[0] = jnp.where(sel, local, -1.0).astype(I32)

    ones8 = jnp.ones((SUBLANES, LANES), BF16)
    tot_t = _dot_nt(ones8, mb)
    triu_strict = jnp.where(rr < rc, 1.0, 0.0).astype(BF16)
    off_t = _dot(tot_t.astype(BF16), triu_strict)
    off_ref[0] = off_t[0:1, :].astype(I32)
    tot_ref[0] = tot_t[0:1, :].astype(I32)

    incl_t = _dot_nt(jnp.where(lane_c <= lane_r, 1.0, 0.0).astype(BF16), mb).astype(BF16)
    p_t = _split_bf16(p.T, 3)
    incl_row = (off + tot)[:, 0:1]
    off_col = off[:, 0:1]
    for c0 in range(0, cap, ct):
        cvec = (c0 + _iota((1, ct), 1)).astype(F32)
        before = incl_row <= cvec
        rsel = jnp.sum(jnp.where(before, 1, 0), axis=0, keepdims=True)
        onehot = _iota((r, ct), 0) == rsel
        onehot_bf = jnp.where(onehot, 1.0, 0.0).astype(BF16)
        g_t = _dot(incl_t, onehot_bf)
        offc = jnp.sum(jnp.where(onehot, off_col, 0.0), axis=0, keepdims=True)
        target = cvec - offc
        lane = jnp.sum(jnp.where(g_t <= target, 1, 0), axis=0, keepdims=True)
        idx_ref[0, :, c0:c0 + ct] = rsel * LANES + lane
        a_t = _dot(p_t[0], onehot_bf) + _dot(p_t[1], onehot_bf) + _dot(p_t[2], onehot_bf)
        hit = _iota((LANES, ct), 0) == lane
        aff_ref[0, :, c0:c0 + ct] = jnp.sum(jnp.where(hit, a_t, 0.0), axis=0, keepdims=True)


def _route(probs_t, cap):
    e, t = probs_t.shape
    r = t // LANES
    ct = min(cap, 512)
    p3 = probs_t.reshape(e, r, LANES)
    return pl.pallas_call(
        functools.partial(_route_kernel, cap=cap, ct=ct),
        grid=(e,),
        in_specs=[pl.BlockSpec((1, r, LANES), lambda i: (i, 0, 0))],
        out_specs=[
            pl.BlockSpec((1, 1, cap), lambda i: (i, 0, 0)),
            pl.BlockSpec((1, 1, cap), lambda i: (i, 0, 0)),
            pl.BlockSpec((1, r, LANES), lambda i: (i, 0, 0)),
            pl.BlockSpec((1, 1, r), lambda i: (i, 0, 0)),
            pl.BlockSpec((1, 1, r), lambda i: (i, 0, 0)),
        ],
        out_shape=[
            jax.ShapeDtypeStruct((e, 1, cap), I32),
            jax.ShapeDtypeStruct((e, 1, cap), F32),
            jax.ShapeDtypeStruct((e, r, LANES), I32),
            jax.ShapeDtypeStruct((e, 1, r), I32),
            jax.ShapeDtypeStruct((e, 1, r), I32),
        ],
        compiler_params=_cparams(("parallel",)),
        name="route",
    )(p3)


def _gather_kernel(idx_ref, h_hbm, out_ref, sem):
    rows = out_ref.shape[0]

    def issue(i, carry):
        tok = idx_ref[0, 0, i]
        pltpu.make_async_copy(h_hbm.at[pl.ds(tok, 1), :], out_ref.at[pl.ds(i, 1), :], sem).start()
        return carry

    lax.fori_loop(0, rows, issue, 0)
    pltpu.make_async_copy(h_hbm.at[pl.ds(0, rows), :], out_ref, sem).wait()


def _gather(h2, idx_flat, tg):
    t, d = h2.shape
    nrows = idx_flat.shape[0]
    idx3 = idx_flat.reshape(nrows // tg, 1, tg)
    return pl.pallas_call(
        _gather_kernel,
        grid=(nrows // tg,),
        in_specs=[
            pl.BlockSpec((1, 1, tg), lambda i: (i, 0, 0), memory_space=pltpu.SMEM),
            pl.BlockSpec(memory_space=pl.ANY),
        ],
        out_specs=pl.BlockSpec((tg, d), lambda i: (i, 0)),
        out_shape=jax.ShapeDtypeStruct((nrows, d), h2.dtype),
        scratch_shapes=[pltpu.SemaphoreType.DMA],
        compiler_params=_cparams(("arbitrary",)),
        name="gather",
    )(idx3, h2)


def _ffn_kernel(xe_ref, aff_ref, wg_ref, wu_ref, wd_ref, out_ref, xb_ref, hid_ref, *, fc):
    xb_ref[...] = xe_ref[...].astype(BF16)
    for c in range(wg_ref.shape[1] // fc):
        g = _dot(xb_ref[...], wg_ref[:, c * fc:(c + 1) * fc])
        u = _dot(xb_ref[...], wu_ref[:, c * fc:(c + 1) * fc])
        hid_ref[:, c * fc:(c + 1) * fc] = (g * _sigmoid(g) * u).astype(BF16)
    y = _dot(hid_ref[...], wd_ref[...])
    for j in range(aff_ref.shape[0]):
        a = jnp.broadcast_to(aff_ref[j:j + 1, :], (LANES, LANES)).T[:, 0:1]
        out_ref[j * LANES:(j + 1) * LANES, :] = y[j * LANES:(j + 1) * LANES, :] * a


def _ffn(xe, aff2d, w_gate, w_up, w_down, cap, rs, fc):
    nrows, d = xe.shape
    e, _, ff = w_gate.shape
    rpe = cap // rs
    return pl.pallas_call(
        functools.partial(_ffn_kernel, fc=fc),
        grid=(e, rpe),
        in_specs=[
            pl.BlockSpec((rs, d), lambda ei, r: (ei * rpe + r, 0)),
            pl.BlockSpec((None, rs // LANES, LANES), lambda ei, r: (ei * rpe + r, 0, 0)),
            pl.BlockSpec((None, d, ff), lambda ei, r: (ei, 0, 0)),
            pl.BlockSpec((None, d, ff), lambda ei, r: (ei, 0, 0)),
            pl.BlockSpec((None, ff, d), lambda ei, r: (ei, 0, 0)),
        ],
        out_specs=pl.BlockSpec((rs, d), lambda ei, r: (ei * rpe + r, 0)),
        out_shape=jax.ShapeDtypeStruct((nrows, d), F32),
        scratch_shapes=[pltpu.VMEM((rs, d), BF16), pltpu.VMEM((rs, ff), BF16)],
        compiler_params=_cparams(("parallel", "arbitrary")),
        name="expert_ffn",
    )(xe, aff2d.reshape(nrows // rs, rs // LANES, LANES), w_gate, w_up, w_down)


def _combine_kernel(off_ref, tot_ref, code_ref, x1_ref, fw_ref, ye_hbm, out_ref, buf_ref, sem,
                    *, cap, n_experts, qc, final_norm):
    blk = pl.program_id(0)
    nblk = pl.num_programs(0)
    tb = x1_ref.shape[0]

    def layout(b):
        qbase = jnp.int32(0)
        chunks, shifts = [], []
        for e in range(n_experts):
            s = off_ref[e, b]
            n = tot_ref[e, b]
            c_lo = s // SUBLANES
            nch = jnp.where(n > 0, (s + n - 1) // SUBLANES - c_lo + 1, 0)
            chunks.append((c_lo, nch, qbase))
            shifts.append(qbase + (s - c_lo * SUBLANES))
            qbase = qbase + nch * SUBLANES
        return chunks, shifts, qbase

    def chunk_copy(src, dst, slot):
        return pltpu.make_async_copy(ye_hbm.at[pl.ds(src, SUBLANES), :],
                                     buf_ref.at[slot, pl.ds(dst, SUBLANES), :], sem.at[slot])

    def fetch(b, slot):
        chunks, _, _ = layout(b)
        for e, (c_lo, nch, qbase) in enumerate(chunks):
            def issue(i, carry, e=e, c_lo=c_lo, qbase=qbase):
                src = pl.multiple_of(e * cap + (c_lo + i) * SUBLANES, SUBLANES)
                dst = pl.multiple_of(qbase + i * SUBLANES, SUBLANES)
                chunk_copy(src, dst, slot).start()
                return carry

            lax.fori_loop(0, nch, issue, 0)

    slot = blk % 2

    @pl.when(blk == 0)
    def _():
        buf_ref[...] = jnp.zeros_like(buf_ref)
        fetch(blk, slot)

    @pl.when(blk + 1 < nblk)
    def _():
        fetch(blk + 1, 1 - slot)

    _, shifts, qtot = layout(blk)

    def drain(i, carry):
        chunk_copy(0, 0, slot).wait()
        return carry

    lax.fori_loop(0, qtot // SUBLANES, drain, 0)

    code_t = code_ref[:, 0, :].astype(F32).T
    nq = (qtot + qc - 1) // qc

    def seg_sum(k, acc):
        q0 = pl.multiple_of(k * qc, qc)
        qi = q0 + _iota((tb, qc), 1)
        s_mat = jnp.zeros((tb, qc), F32)
        for e in range(n_experts):
            ce = code_t[:, e:e + 1].astype(I32)
            tgt = jnp.where(ce >= 0, ce + shifts[e], -1)
            s_mat = s_mat + jnp.where(tgt == qi, 1.0, 0.0)
        rows = buf_ref[slot, pl.ds(q0, qc), :]
        hi, lo = _split_bf16(rows, 2)
        sb = s_mat.astype(BF16)
        return acc + _dot(sb, hi) + _dot(sb, lo)

    ffn = lax.fori_loop(0, nq, seg_sum, jnp.zeros(x1_ref.shape, F32))
    x2 = x1_ref[...] + ffn
    if final_norm:
        x2 = x2 * lax.rsqrt(jnp.mean(x2 * x2, axis=-1, keepdims=True) + EPS) * fw_ref[...]
    out_ref[...] = x2


def _combine(off, tot, code, x1, ye, final_w, cap, final_norm):
    t, d = x1.shape
    e = code.shape[0]
    tb = LANES
    qc = 256
    max_rows = e * (tb + 2 * SUBLANES)
    buf_rows = ((max_rows + qc - 1) // qc) * qc
    return pl.pallas_call(
        functools.partial(_combine_kernel, cap=cap, n_experts=e, qc=qc, final_norm=final_norm),
        grid_spec=pltpu.PrefetchScalarGridSpec(
            num_scalar_prefetch=2,
            grid=(t // tb,),
            in_specs=[
                pl.BlockSpec((e, None, 1, tb), lambda i, *_: (0, i, 0, 0)),
                pl.BlockSpec((tb, d), lambda i, *_: (i, 0)),
                pl.BlockSpec((1, d), lambda i, *_: (0, 0)),
                pl.BlockSpec(memory_space=pl.ANY),
            ],
            out_specs=pl.BlockSpec((tb, d), lambda i, *_: (i, 0)),
            scratch_shapes=[pltpu.VMEM((2, buf_rows, d), F32), pltpu.SemaphoreType.DMA((2,))],
        ),
        out_shape=jax.ShapeDtypeStruct((t, d), F32),
        compiler_params=_cparams(("arbitrary",)),
        name="combine",
    )(off, tot, code.reshape(e, t // tb, 1, tb), x1, final_w, ye)


def _pick(n, prefs):
    for p in prefs:
        if n % p == 0:
            return p
    raise ValueError(f"no tile size for {n}")


def _layer(x, lw, final_w, final_norm):
    batch, seq, d = x.shape
    t = batch * seq
    heads = lw["dt_bias_f"].shape[-1]
    d_inner = lw["ssm_norm_w"].shape[-1]
    conv_ch = lw["conv_w"].shape[-1]
    n_experts = lw["w_router"].shape[-1]
    cap = CAPACITY_FACTOR * t // n_experts
    assert seq % CHUNK == 0 and cap % LANES == 0 and 2 * heads <= LANES

    x2d = x.reshape(t, d)
    main, dt_raw = _in_proj(x2d, lw["norm_mix_w"], lw["w_main"], lw["w_dt"],
                            _pick(t, (1024, 512, 256, 128)), _pick(lw["w_main"].shape[1], (1024, 512)))
    xc = _conv(main, lw["conv_w"], lw["conv_b"], seq, d_inner,
               _pick(seq, (512, 256, 128)), _pick(conv_ch, (1024, 512)))
    yf = _ssd(xc, dt_raw, lw["dt_bias_all"], lw["a_log_all"], lw["d_skip_exp"],
              batch, seq, d_inner, heads, reverse=False)
    yb = _ssd(xc, dt_raw, lw["dt_bias_all"], lw["a_log_all"], lw["d_skip_exp"],
              batch, seq, d_inner, heads, reverse=True)
    x1, h2, probs_t = _mix(yf, yb, main, x2d, lw["ssm_norm_w"], lw["w_ssm_out"], lw["w_pool"],
                           lw["pool_scale"], lw["w_o"], lw["norm_ffn_w"], lw["w_router_pad"],
                           seq, d_inner, conv_ch, n_experts, _pick(seq, (256, 128)))
    idx, aff, code, off, tot = _route(probs_t, cap)
    xe = _gather(h2, idx.reshape(n_experts * cap), _pick(cap, (512, 256, 128)))
    ye = _ffn(xe, aff.reshape(n_experts * cap // LANES, LANES), lw["w_gate"], lw["w_up"], lw["w_down"],
              cap, _pick(cap, (512, 256, 128)), _pick(lw["w_gate"].shape[-1], (256, 128)))
    r = t // LANES
    out = _combine(off.reshape(n_experts, r), tot.reshape(n_experts, r), code, x1, ye, final_w,
                   cap, final_norm)
    return out.reshape(batch, seq, d)


def _prep_layer(i, norm_mix_w, w_in, conv_w, conv_b, dt_bias_f, dt_bias_b, a_log_f, a_log_b, d_skip,
                ssm_norm_w, w_ssm_out, w_pool, pool_scale, w_o, norm_ffn_w, w_router, w_gate, w_up, w_down):
    d = w_in.shape[1]
    heads = dt_bias_f.shape[-1]
    d_inner = ssm_norm_w.shape[-1]
    conv_ch = conv_w.shape[-1]
    p = d_inner // heads
    o2 = d_inner + conv_ch
    o4 = o2 + 2 * heads
    wi = w_in[i]
    w_main = jnp.concatenate([wi[:, :o2], wi[:, o4:]], axis=1).astype(BF16)
    w_dt = jnp.pad(wi[:, o2:o4], ((0, 0), (0, LANES - 2 * heads)))
    row = lambda v: v.reshape(1, -1).astype(F32)
    padl = lambda v: jnp.pad(v.reshape(1, -1).astype(F32), ((0, 0), (0, LANES - v.size)))
    return {
        "norm_mix_w": row(norm_mix_w[i]), "w_main": w_main, "w_dt": w_dt,
        "conv_w": conv_w[i], "conv_b": row(conv_b[i]),
        "dt_bias_f": dt_bias_f[i],
        "dt_bias_all": padl(jnp.concatenate([dt_bias_f[i], dt_bias_b[i]])),
        "a_log_all": padl(jnp.concatenate([a_log_f[i], a_log_b[i]])),
        "d_skip_exp": row(jnp.repeat(d_skip[i], p)),
        "ssm_norm_w": row(ssm_norm_w[i]), "w_ssm_out": w_ssm_out[i].astype(BF16),
        "w_pool": w_pool[i].astype(BF16), "pool_scale": row(pool_scale[i]),
        "w_o": w_o[i].astype(BF16), "norm_ffn_w": row(norm_ffn_w[i]),
        "w_router": w_router[i],
        "w_router_pad": jnp.pad(w_router[i].astype(F32), ((0, 0), (0, LANES - w_router.shape[-1]))),
        "w_gate": w_gate[i].astype(BF16), "w_up": w_up[i].astype(BF16), "w_down": w_down[i].astype(BF16),
    }


def kernel(x_prompt, x_sample, norm_mix_w, w_in, conv_w, conv_b, dt_bias_f, dt_bias_b, a_log_f, a_log_b,
           d_skip, ssm_norm_w, w_ssm_out, w_pool, pool_scale, w_o, norm_ffn_w, w_router, w_gate, w_up,
           w_down, norm_final_w):
    depth = w_in.shape[0]
    layers = [_prep_layer(i, norm_mix_w, w_in, conv_w, conv_b, dt_bias_f, dt_bias_b, a_log_f, a_log_b,
                          d_skip, ssm_norm_w, w_ssm_out, w_pool, pool_scale, w_o, norm_ffn_w, w_router,
                          w_gate, w_up, w_down) for i in range(depth)]
    final_w = norm_final_w.reshape(1, -1).astype(F32)

    def trunk(x):
        for i, lw in enumerate(layers):
            x = _layer(x, lw, final_w, final_norm=(i == depth - 1))
        return x

    return (trunk(x_prompt), trunk(x_sample))
```

```python
import functools

import jax
import jax.numpy as jnp
from jax import lax
from jax.experimental import pallas as pl
from jax.experimental.pallas import tpu as pltpu

F32 = jnp.float32
BF16 = jnp.bfloat16
I32 = jnp.int32

EPS = 1e-6
CHUNK = 128
LANES = 128
SUBLANES = 8
BF16_ROWS = 16
SSM_GROUPS = 8
POOL_WINDOWS = (2, 4, 8, 16)
CAPACITY_FACTOR = 2
VMEM_LIMIT = 56 * 1024 * 1024


def _cparams(sem):
    return pltpu.CompilerParams(dimension_semantics=sem, vmem_limit_bytes=VMEM_LIMIT)


def _iota(shape, dim):
    return lax.broadcasted_iota(I32, shape, dim)


def _sigmoid(x):
    return 1.0 / (1.0 + jnp.exp(-x))


def _split_bf16(x, terms):
    parts = []
    rem = x
    for _ in range(terms):
        p = rem.astype(BF16)
        parts.append(p)
        rem = rem - p.astype(F32)
    return parts


def _dot(a, b):
    return jnp.dot(a, b, preferred_element_type=F32)


def _dot_nt(a, b):
    return lax.dot_general(a, b, (((1,), (1,)), ((), ())), preferred_element_type=F32)


def _dot_exact_lhs(mask_bf, x, terms=3):
    out = None
    for p in _split_bf16(x, terms):
        d = _dot(mask_bf, p)
        out = d if out is None else out + d
    return out


def _in_proj_kernel(x_ref, nw_ref, w_ref, wdt_ref, bias_ref, main_ref, dt_ref, h_ref, *, tn):
    x = x_ref[...]
    h = x * lax.rsqrt(jnp.mean(x * x, axis=-1, keepdims=True) + EPS) * nw_ref[...]
    hb, hl = _split_bf16(h, 2)
    h_ref[...] = hb
    wb, wl = _split_bf16(wdt_ref[...], 2)
    v = _dot(hb, wb) + _dot(hl, wb) + _dot(hb, wl) + bias_ref[...]
    dt_ref[...] = jnp.maximum(v, 0.0) + jnp.log1p(jnp.exp(-jnp.abs(v)))
    for j in range(w_ref.shape[1] // tn):
        main_ref[:, j * tn:(j + 1) * tn] = _dot(h_ref[...], w_ref[:, j * tn:(j + 1) * tn]).astype(main_ref.dtype)


def _in_proj(x2d, norm_w, w_main, w_dt, dt_bias, tm, tn):
    t, d = x2d.shape
    ncols = w_main.shape[1]
    const = lambda shape: pl.BlockSpec(shape, lambda i: (0, 0), pipeline_mode=pl.Buffered(1))
    return pl.pallas_call(
        functools.partial(_in_proj_kernel, tn=tn),
        grid=(t // tm,),
        in_specs=[
            pl.BlockSpec((tm, d), lambda i: (i, 0)),
            const((1, d)),
            const((d, ncols)),
            const((d, LANES)),
            const((1, LANES)),
        ],
        out_specs=[
            pl.BlockSpec((tm, ncols), lambda i: (i, 0)),
            pl.BlockSpec((tm, LANES), lambda i: (i, 0)),
        ],
        out_shape=[
            jax.ShapeDtypeStruct((t, ncols), BF16),
            jax.ShapeDtypeStruct((t, LANES), F32),
        ],
        scratch_shapes=[pltpu.VMEM((tm, d), BF16)],
        compiler_params=_cparams(("parallel",)),
        name="in_proj",
    )(x2d, norm_w, w_main, w_dt, dt_bias)


def _conv_kernel(cur_ref, prev_ref, next_ref, w_ref, b_ref, out_ref, buf_ref, *, tiles_per_seq):
    i = pl.program_id(0)
    tr = cur_ref.shape[0]
    halo = prev_ref.shape[0]
    pos = i % tiles_per_seq
    keep_prev = jnp.where(pos == 0, 0.0, 1.0)
    keep_next = jnp.where(pos == tiles_per_seq - 1, 0.0, 1.0)
    buf_ref[0:halo, :] = (prev_ref[...].astype(F32) * keep_prev).astype(BF16)
    buf_ref[halo:halo + tr, :] = cur_ref[...]
    buf_ref[halo + tr:halo + tr + halo, :] = (next_ref[...].astype(F32) * keep_next).astype(BF16)
    k = w_ref.shape[0]
    pad = k // 2
    win = CHUNK + 2 * halo
    ti = _iota((CHUNK, win), 0)
    ji = _iota((CHUNK, win), 1)
    shift = {j: jnp.where(ji == ti + halo + (j - pad), 1.0, 0.0).astype(BF16) for j in range(k) if j != pad}
    for b in range(tr // CHUNK):
        xw = buf_ref[b * CHUNK:b * CHUNK + win, :]
        acc = b_ref[...] + w_ref[pad:pad + 1, :] * cur_ref[b * CHUNK:(b + 1) * CHUNK, :].astype(F32)
        for j in shift:
            acc = acc + w_ref[j:j + 1, :] * _dot(shift[j], xw)
        out_ref[b * CHUNK:(b + 1) * CHUNK, :] = (acc * _sigmoid(acc)).astype(out_ref.dtype)


def _conv(main, conv_w, conv_b, seq, col0, tr, tc):
    t = main.shape[0]
    k, ch = conv_w.shape
    halo = BF16_ROWS
    cb0 = col0 // tc
    nhalo = t // halo
    rh = tr // halo
    return pl.pallas_call(
        functools.partial(_conv_kernel, tiles_per_seq=seq // tr),
        grid=(t // tr, ch // tc),
        in_specs=[
            pl.BlockSpec((tr, tc), lambda i, j: (i, cb0 + j)),
            pl.BlockSpec((halo, tc), lambda i, j: (jnp.maximum(i * rh - 1, 0), cb0 + j)),
            pl.BlockSpec((halo, tc), lambda i, j: (jnp.minimum((i + 1) * rh, nhalo - 1), cb0 + j)),
            pl.BlockSpec((k, tc), lambda i, j: (0, j)),
            pl.BlockSpec((1, tc), lambda i, j: (0, j)),
        ],
        out_specs=pl.BlockSpec((tr, tc), lambda i, j: (i, j)),
        out_shape=jax.ShapeDtypeStruct((t, ch), BF16),
        scratch_shapes=[pltpu.VMEM((tr + 2 * halo, tc), BF16)],
        compiler_params=_cparams(("parallel", "parallel")),
        name="conv_silu",
    )(main, main, main, conv_w, conv_b)


def _ssd_kernel(xs_ref, b_ref, c_ref, dt_ref, alog_ref, dskip_ref, y_ref, state_ref,
                *, reverse, heads, add_skip):
    @pl.when(pl.program_id(1) == 0)
    def _():
        state_ref[...] = jnp.zeros_like(state_ref)

    L = xs_ref.shape[0]
    d_inner = xs_ref.shape[1]
    p = d_inner // heads
    hpg = heads // SSM_GROUPS
    n = b_ref.shape[1] // SSM_GROUPS
    assert n == L
    assert hpg % 2 == 0
    off = heads if reverse else 0
    first = _iota((L, 2 * p), 1) < p

    dt = dt_ref[...]
    da = dt * (-jnp.exp(alog_ref[...]))
    row = _iota((L, L), 0)
    col = _iota((L, L), 1)
    lower = col <= row
    upper = col >= row
    mask = upper if reverse else lower
    tri = jnp.where(mask, 1.0, 0.0).astype(BF16)
    acum = _dot_exact_lhs(tri, da)
    acum_t = acum.T
    ldt_t = jnp.log(dt).T
    last = 0 if reverse else L - 1
    a_last_t = acum_t[:, last:last + 1]
    src_t = acum_t - ldt_t
    wrow_t = jnp.exp(a_last_t - src_t)
    dchunk_t = jnp.exp(a_last_t)

    for g in range(SSM_GROUPS):
        bg = b_ref[:, g * n:(g + 1) * n]
        cg = c_ref[:, g * n:(g + 1) * n]
        cgf = cg.astype(F32)
        cb = _dot_nt(cg, bg)
        btf = bg.astype(F32).T
        for j in range(0, hpg, 2):
            hh0 = g * hpg + j
            cols = slice(hh0 * p, (hh0 + 2) * p)
            xpair = xs_ref[:, cols]
            st = state_ref[hh0 // 2]
            rhs = jnp.concatenate([xpair, st.astype(BF16)], axis=0)
            ys, ds, dch = [], [], []
            for hh in (hh0, hh0 + 1):
                h = off + hh
                colb = jnp.broadcast_to(acum[:, h:h + 1], (L, L))
                m = (cb * jnp.exp(jnp.where(mask, colb - src_t[h:h + 1, :], -1e30))).astype(BF16)
                cs = (cgf * jnp.exp(colb)).astype(BF16)
                ys.append(_dot(jnp.concatenate([m, cs], axis=1), rhs))
                bts = (btf * wrow_t[h:h + 1, :]).astype(BF16)
                ds.append(_dot(bts, xpair))
                dch.append(dchunk_t[h:h + 1, :])
            y = jnp.where(first, ys[0], ys[1])
            state_ref[hh0 // 2] = (st * jnp.where(first[0:1, :], dch[0], dch[1])
                                   + jnp.where(first, ds[0], ds[1]))
            if add_skip:
                y = y + xpair.astype(F32) * dskip_ref[:, cols]
            y_ref[:, cols] = y.astype(y_ref.dtype)


def _ssd(xc, dt, alog_all, dskip_exp, batch, seq, d_inner, heads, reverse):
    t = xc.shape[0]
    nc = seq // CHUNK
    gn = (xc.shape[1] - d_inner) // 2
    bblk = d_inner // gn
    n = gn // SSM_GROUPS

    def rowblk(b, c):
        return b * nc + (nc - 1 - c if reverse else c)

    return pl.pallas_call(
        functools.partial(_ssd_kernel, reverse=reverse, heads=heads, add_skip=not reverse),
        grid=(batch, nc),
        in_specs=[
            pl.BlockSpec((CHUNK, d_inner), lambda b, c: (rowblk(b, c), 0)),
            pl.BlockSpec((CHUNK, gn), lambda b, c: (rowblk(b, c), bblk)),
            pl.BlockSpec((CHUNK, gn), lambda b, c: (rowblk(b, c), bblk + 1)),
            pl.BlockSpec((CHUNK, LANES), lambda b, c: (rowblk(b, c), 0)),
            pl.BlockSpec((1, LANES), lambda b, c: (0, 0)),
            pl.BlockSpec((1, d_inner), lambda b, c: (0, 0)),
        ],
        out_specs=pl.BlockSpec((CHUNK, d_inner), lambda b, c: (rowblk(b, c), 0)),
        out_shape=jax.ShapeDtypeStruct((t, d_inner), BF16),
        scratch_shapes=[pltpu.VMEM((heads // 2, n, 2 * d_inner // heads), F32)],
        compiler_params=_cparams(("parallel", "arbitrary")),
        name="ssd_bwd" if reverse else "ssd_fwd",
    )(xc, xc, xc, dt, alog_all, dskip_exp)


def _mix_kernel(yf_ref, yb_ref, z_ref, u_ref, up_ref, un_ref, ga_ref, gb_ref, x_ref,
                snw_ref, wso_ref, wp_ref, ps_ref, wo_ref, fnw_ref, wr_ref,
                x1_ref, h2_ref, pt_ref, *, tiles_per_seq, seq, n_experts):
    tm = x_ref.shape[0]
    d_inner = z_ref.shape[1]
    gw = d_inner // SSM_GROUPS
    halo = up_ref.shape[0]
    sub = min(tm, 2 * CHUNK)
    pos0 = (pl.program_id(0) % tiles_per_seq) * tm
    pw = u_ref.shape[1] // len(POOL_WINDOWS)
    keep_prev = jnp.where(pos0 > 0, 1.0, 0.0)
    keep_next = jnp.where(pos0 + tm < seq, 1.0, 0.0)
    wb, wl = _split_bf16(wr_ref[...], 2)

    tq = _iota((sub, sub), 0)
    sq = _iota((sub, sub), 1)
    tp = _iota((sub, halo), 0)
    jp = _iota((sub, halo), 1)
    bands = []
    for win in POOL_WINDOWS:
        lo = win // 2
        hi = win - lo
        in_c = (sq >= tq - lo) & (sq < tq + hi)
        in_p = jp - halo >= tp - lo
        in_n = jp + sub < tp + hi
        bands.append((jnp.where(in_c, 1.0, 0.0).astype(BF16), in_p, in_n))

    for si in range(tm // sub):
        rows = slice(si * sub, (si + 1) * sub)
        at_top = si == 0
        at_bottom = si == tm // sub - 1

        z = z_ref[rows, :].astype(F32)
        y = (yf_ref[rows, :].astype(F32) + yb_ref[rows, :].astype(F32)) * (z * _sigmoid(z))
        parts = []
        for g in range(SSM_GROUPS):
            blk = y[:, g * gw:(g + 1) * gw]
            sc = lax.rsqrt(jnp.mean(blk * blk, axis=-1, keepdims=True) + EPS)
            parts.append((blk * sc * snw_ref[:, g * gw:(g + 1) * gw]).astype(BF16))
        y_a = _dot(jnp.concatenate(parts, axis=1), wso_ref[...])

        tpos = pos0 + si * sub + _iota((sub, 1), 0)
        pooled = []
        for gi, win in enumerate(POOL_WINDOWS):
            lo = win // 2
            hi = win - lo
            band_c, in_p, in_n = bands[gi]
            cg = slice(gi * pw, (gi + 1) * pw)
            ug = u_ref[rows, cg]
            u_prev = up_ref[:, cg] if at_top else u_ref[si * sub - halo:si * sub, cg]
            u_next = un_ref[:, cg] if at_bottom else u_ref[(si + 1) * sub:(si + 1) * sub + halo, cg]
            band_p = jnp.where(in_p, keep_prev if at_top else 1.0, 0.0).astype(BF16)
            band_n = jnp.where(in_n, keep_next if at_bottom else 1.0, 0.0).astype(BF16)
            s = _dot(band_c, ug) + _dot(band_p, u_prev) + _dot(band_n, u_next)
            cnt = (jnp.minimum(tpos + hi, seq) - jnp.maximum(tpos - lo, 0)).astype(F32)
            pg = (s / cnt - ug.astype(F32)).astype(BF16)
            pooled.append(_dot(pg, wp_ref[gi]))
        y_b = jnp.concatenate(pooled, axis=1) * ps_ref[...]

        merged = (_sigmoid(ga_ref[rows, :].astype(F32)) * y_a
                  + _sigmoid(gb_ref[rows, :].astype(F32)) * y_b)
        x1 = x_ref[rows, :] + _dot(merged.astype(BF16), wo_ref[...])
        x1_ref[rows, :] = x1

        h2 = x1 * lax.rsqrt(jnp.mean(x1 * x1, axis=-1, keepdims=True) + EPS) * fnw_ref[...]
        h2_ref[rows, :] = h2
        hb, hl = _split_bf16(h2, 2)
        logits = _dot(hb, wb) + _dot(hl, wb) + _dot(hb, wl)
        logits = jnp.where(_iota(logits.shape, 1) < n_experts, logits, -1e30)
        e = jnp.exp(logits - jnp.max(logits, axis=-1, keepdims=True))
        probs = e / jnp.sum(e, axis=-1, keepdims=True)
        pt_ref[:, rows] = probs.T[:n_experts, :]


def _mix(yf, yb, main, x2d, ssm_norm_w, w_ssm_out, w_pool, pool_scale, w_o, norm_ffn_w, w_router_pad,
         seq, d_inner, conv_ch, n_experts, tm):
    t, d = x2d.shape
    halo = BF16_ROWS
    rh = tm // halo
    nhalo = t // halo
    zb = 0
    ub = (d_inner + conv_ch) // d
    full = lambda shape: pl.BlockSpec(shape, lambda i: (0,) * len(shape))
    return pl.pallas_call(
        functools.partial(_mix_kernel, tiles_per_seq=seq // tm, seq=seq, n_experts=n_experts),
        grid=(t // tm,),
        in_specs=[
            pl.BlockSpec((tm, d_inner), lambda i: (i, 0)),
            pl.BlockSpec((tm, d_inner), lambda i: (i, 0)),
            pl.BlockSpec((tm, d_inner), lambda i: (i, zb)),
            pl.BlockSpec((tm, d), lambda i: (i, ub)),
            pl.BlockSpec((halo, d), lambda i: (jnp.maximum(i * rh - 1, 0), ub)),
            pl.BlockSpec((halo, d), lambda i: (jnp.minimum((i + 1) * rh, nhalo - 1), ub)),
            pl.BlockSpec((tm, d), lambda i: (i, ub + 1)),
            pl.BlockSpec((tm, d), lambda i: (i, ub + 2)),
            pl.BlockSpec((tm, d), lambda i: (i, 0)),
            full((1, d_inner)),
            full((d_inner, d)),
            full(w_pool.shape),
            full((1, d)),
            full((d, d)),
            full((1, d)),
            full((d, LANES)),
        ],
        out_specs=[
            pl.BlockSpec((tm, d), lambda i: (i, 0)),
            pl.BlockSpec((tm, d), lambda i: (i, 0)),
            pl.BlockSpec((n_experts, tm), lambda i: (0, i)),
        ],
        out_shape=[
            jax.ShapeDtypeStruct((t, d), F32),
            jax.ShapeDtypeStruct((t, d), F32),
            jax.ShapeDtypeStruct((n_experts, t), F32),
        ],
        compiler_params=_cparams(("parallel",)),
        name="mix",
    )(yf, yb, main, main, main, main, main, main, x2d,
      ssm_norm_w, w_ssm_out, w_pool, pool_scale, w_o, norm_ffn_w, w_router_pad)


def _route_kernel(p_ref, idx_ref, aff_ref, code_ref, off_ref, tot_ref, *, cap, ct):
    p = p_ref[0]
    r = p.shape[0]
    key = pltpu.bitcast(p, I32)

    def search(i, prefix):
        cand = prefix | lax.shift_left(jnp.int32(1), 30 - i)
        cnt = jnp.sum(jnp.where(key >= cand, 1, 0))
        return jnp.where(cnt >= cap, cand, prefix)

    tau = lax.fori_loop(0, 31, search, jnp.int32(0))

    lane_r = _iota((LANES, LANES), 0)
    lane_c = _iota((LANES, LANES), 1)
    tri_incl = jnp.where(lane_r <= lane_c, 1.0, 0.0).astype(BF16)
    rr = _iota((r, r), 0)
    rc = _iota((r, r), 1)
    tril_strict = jnp.where(rc < rr, 1.0, 0.0).astype(BF16)

    def cumsum(mask):
        mb = jnp.where(mask, 1.0, 0.0).astype(BF16)
        incl = _dot(mb, tri_incl)
        tot = jnp.broadcast_to(incl[:, LANES - 1:LANES], (r, LANES))
        off = _dot(tril_strict, tot.astype(BF16))
        return mb, incl, off, tot

    gt = key > tau
    eq = key == tau
    need = (cap - jnp.sum(jnp.where(gt, 1, 0))).astype(F32)
    mb_eq, incl_eq, off_eq, _ = cumsum(eq)
    rank_eq = off_eq + incl_eq - mb_eq.astype(F32)
    sel = gt | (eq & (rank_eq < need))
    mb, incl, off, tot = cumsum(sel)

    local = incl - mb.astype(F32)
    code_ref[0] = jnp.where(sel, local, -1.0).astype(I32)

    ones8 = jnp.ones((SUBLANES, LANES), BF16)
    tot_t = _dot_nt(ones8, mb)
    triu_strict = jnp.where(rr < rc, 1.0, 0.0).astype(BF16)
    off_t = _dot(tot_t.astype(BF16), triu_strict)
    off_ref[0] = off_t[0:1, :].astype(I32)
    tot_ref[0] = tot_t[0:1, :].astype(I32)

    incl_t = _dot_nt(jnp.where(lane_c <= lane_r, 1.0, 0.0).astype(BF16), mb).astype(BF16)
    p_t = _split_bf16(p.T, 3)
    incl_row = (off + tot)[:, 0:1]
    off_col = off[:, 0:1]
    for c0 in range(0, cap, ct):
        cvec = (c0 + _iota((1, ct), 1)).astype(F32)
        before = incl_row <= cvec
        rsel = jnp.sum(jnp.where(before, 1, 0), axis=0, keepdims=True)
        onehot = _iota((r, ct), 0) == rsel
        onehot_bf = jnp.where(onehot, 1.0, 0.0).astype(BF16)
        g_t = _dot(incl_t, onehot_bf)
        offc = jnp.sum(jnp.where(onehot, off_col, 0.0), axis=0, keepdims=True)
        target = cvec - offc
        lane = jnp.sum(jnp.where(g_t <= target, 1, 0), axis=0, keepdims=True)
        idx_ref[0, :, c0:c0 + ct] = rsel * LANES + lane
        a_t = _dot(p_t[0], onehot_bf) + _dot(p_t[1], onehot_bf) + _dot(p_t[2], onehot_bf)
        hit = _iota((LANES, ct), 0) == lane
        aff_ref[0, :, c0:c0 + ct] = jnp.sum(jnp.where(hit, a_t, 0.0), axis=0, keepdims=True)


def _route(probs_t, cap):
    e, t = probs_t.shape
    r = t // LANES
    ct = min(cap, 512)
    p3 = probs_t.reshape(e, r, LANES)
    return pl.pallas_call(
        functools.partial(_route_kernel, cap=cap, ct=ct),
        grid=(e,),
        in_specs=[pl.BlockSpec((1, r, LANES), lambda i: (i, 0, 0))],
        out_specs=[
            pl.BlockSpec((1, 1, cap), lambda i: (i, 0, 0)),
            pl.BlockSpec((1, 1, cap), lambda i: (i, 0, 0)),
            pl.BlockSpec((1, r, LANES), lambda i: (i, 0, 0)),
            pl.BlockSpec((1, 1, r), lambda i: (i, 0, 0)),
            pl.BlockSpec((1, 1, r), lambda i: (i, 0, 0)),
        ],
        out_shape=[
            jax.ShapeDtypeStruct((e, 1, cap), I32),
            jax.ShapeDtypeStruct((e, 1, cap), F32),
            jax.ShapeDtypeStruct((e, r, LANES), I32),
            jax.ShapeDtypeStruct((e, 1, r), I32),
            jax.ShapeDtypeStruct((e, 1, r), I32),
        ],
        compiler_params=_cparams(("parallel",)),
        name="route",
    )(p3)


def _gather_kernel(idx_ref, h_hbm, out_ref, sem):
    rows = out_ref.shape[0]

    def issue(i, carry):
        tok = idx_ref[0, 0, i]
        pltpu.make_async_copy(h_hbm.at[pl.ds(tok, 1), :], out_ref.at[pl.ds(i, 1), :], sem).start()
        return carry

    lax.fori_loop(0, rows, issue, 0)
    pltpu.make_async_copy(h_hbm.at[pl.ds(0, rows), :], out_ref, sem).wait()


def _gather(h2, idx_flat, tg):
    t, d = h2.shape
    nrows = idx_flat.shape[0]
    idx3 = idx_flat.reshape(nrows // tg, 1, tg)
    return pl.pallas_call(
        _gather_kernel,
        grid=(nrows // tg,),
        in_specs=[
            pl.BlockSpec((1, 1, tg), lambda i: (i, 0, 0), memory_space=pltpu.SMEM),
            pl.BlockSpec(memory_space=pl.ANY),
        ],
        out_specs=pl.BlockSpec((tg, d), lambda i: (i, 0)),
        out_shape=jax.ShapeDtypeStruct((nrows, d), h2.dtype),
        scratch_shapes=[pltpu.SemaphoreType.DMA],
        compiler_params=_cparams(("arbitrary",)),
        name="gather",
    )(idx3, h2)


def _ffn_kernel(xe_ref, aff_ref, wg_ref, wu_ref, wd_ref, out_ref, xb_ref, hid_ref, *, fc):
    xb_ref[...] = xe_ref[...].astype(BF16)
    for c in range(wg_ref.shape[1] // fc):
        g = _dot(xb_ref[...], wg_ref[:, c * fc:(c + 1) * fc])
        u = _dot(xb_ref[...], wu_ref[:, c * fc:(c + 1) * fc])
        hid_ref[:, c * fc:(c + 1) * fc] = (g * _sigmoid(g) * u).astype(BF16)
    y = _dot(hid_ref[...], wd_ref[...])
    for j in range(aff_ref.shape[0]):
        a = jnp.broadcast_to(aff_ref[j:j + 1, :], (LANES, LANES)).T[:, 0:1]
        out_ref[j * LANES:(j + 1) * LANES, :] = (y[j * LANES:(j + 1) * LANES, :] * a).astype(out_ref.dtype)


def _ffn(xe, aff2d, w_gate, w_up, w_down, cap, rs, fc):
    nrows, d = xe.shape
    e, _, ff = w_gate.shape
    rpe = cap // rs
    return pl.pallas_call(
        functools.partial(_ffn_kernel, fc=fc),
        grid=(e, rpe),
        in_specs=[
            pl.BlockSpec((rs, d), lambda ei, r: (ei * rpe + r, 0)),
            pl.BlockSpec((None, rs // LANES, LANES), lambda ei, r: (ei * rpe + r, 0, 0)),
            pl.BlockSpec((None, d, ff), lambda ei, r: (ei, 0, 0)),
            pl.BlockSpec((None, d, ff), lambda ei, r: (ei, 0, 0)),
            pl.BlockSpec((None, ff, d), lambda ei, r: (ei, 0, 0)),
        ],
        out_specs=pl.BlockSpec((rs, d), lambda ei, r: (ei * rpe + r, 0)),
        out_shape=jax.ShapeDtypeStruct((nrows, d), BF16),
        scratch_shapes=[pltpu.VMEM((rs, d), BF16), pltpu.VMEM((rs, ff), BF16)],
        compiler_params=_cparams(("parallel", "arbitrary")),
        name="expert_ffn",
    )(xe, aff2d.reshape(nrows // rs, rs // LANES, LANES), w_gate, w_up, w_down)


def _combine_kernel(off_ref, tot_ref, code_ref, x1_ref, fw_ref, ye_hbm, out_ref, buf_ref, sem,
                    *, cap, n_experts, qc, final_norm):
    blk = pl.program_id(0)
    nblk = pl.num_programs(0)
    tb = x1_ref.shape[0]

    ch = BF16_ROWS

    def layout(b):
        qbase = jnp.int32(0)
        chunks, shifts = [], []
        for e in range(n_experts):
            s = off_ref[e, b]
            n = tot_ref[e, b]
            c_lo = s // ch
            nch = jnp.where(n > 0, (s + n - 1) // ch - c_lo + 1, 0)
            chunks.append((c_lo, nch, qbase))
            shifts.append(qbase + (s - c_lo * ch))
            qbase = qbase + nch * ch
        return chunks, shifts, qbase

    def chunk_copy(src, dst, slot):
        return pltpu.make_async_copy(ye_hbm.at[pl.ds(src, ch), :],
                                     buf_ref.at[slot, pl.ds(dst, ch), :], sem.at[slot])

    def fetch(b, slot):
        chunks, _, _ = layout(b)
        for e, (c_lo, nch, qbase) in enumerate(chunks):
            def issue(i, carry, e=e, c_lo=c_lo, qbase=qbase):
                src = pl.multiple_of(e * cap + (c_lo + i) * ch, ch)
                dst = pl.multiple_of(qbase + i * ch, ch)
                chunk_copy(src, dst, slot).start()
                return carry

            lax.fori_loop(0, nch, issue, 0)

    slot = blk % 2

    @pl.when(blk == 0)
    def _():
        buf_ref[...] = jnp.zeros_like(buf_ref)
        fetch(blk, slot)

    @pl.when(blk + 1 < nblk)
    def _():
        fetch(blk + 1, 1 - slot)

    _, shifts, qtot = layout(blk)

    def drain(i, carry):
        chunk_copy(0, 0, slot).wait()
        return carry

    lax.fori_loop(0, qtot // ch, drain, 0)

    code_t = code_ref[:, 0, :].astype(F32).T
    nq = (qtot + qc - 1) // qc

    def seg_sum(k, acc):
        q0 = pl.multiple_of(k * qc, qc)
        qi = q0 + _iota((tb, qc), 1)
        s_mat = jnp.zeros((tb, qc), F32)
        for e in range(n_experts):
            ce = code_t[:, e:e + 1].astype(I32)
            tgt = jnp.where(ce >= 0, ce + shifts[e], -1)
            s_mat = s_mat + jnp.where(tgt == qi, 1.0, 0.0)
        return acc + _dot(s_mat.astype(BF16), buf_ref[slot, pl.ds(q0, qc), :])

    ffn = lax.fori_loop(0, nq, seg_sum, jnp.zeros(x1_ref.shape, F32))
    x2 = x1_ref[...] + ffn
    if final_norm:
        x2 = x2 * lax.rsqrt(jnp.mean(x2 * x2, axis=-1, keepdims=True) + EPS) * fw_ref[...]
    out_ref[...] = x2


def _combine(off, tot, code, x1, ye, final_w, cap, final_norm):
    t, d = x1.shape
    e = code.shape[0]
    tb = LANES
    qc = 256
    max_rows = e * (tb + 2 * BF16_ROWS)
    buf_rows = ((max_rows + qc - 1) // qc) * qc
    return pl.pallas_call(
        functools.partial(_combine_kernel, cap=cap, n_experts=e, qc=qc, final_norm=final_norm),
        grid_spec=pltpu.PrefetchScalarGridSpec(
            num_scalar_prefetch=2,
            grid=(t // tb,),
            in_specs=[
                pl.BlockSpec((e, None, 1, tb), lambda i, *_: (0, i, 0, 0)),
                pl.BlockSpec((tb, d), lambda i, *_: (i, 0)),
                pl.BlockSpec((1, d), lambda i, *_: (0, 0)),
                pl.BlockSpec(memory_space=pl.ANY),
            ],
            out_specs=pl.BlockSpec((tb, d), lambda i, *_: (i, 0)),
            scratch_shapes=[pltpu.VMEM((2, buf_rows, d), ye.dtype), pltpu.SemaphoreType.DMA((2,))],
        ),
        out_shape=jax.ShapeDtypeStruct((t, d), F32),
        compiler_params=_cparams(("arbitrary",)),
        name="combine",
    )(off, tot, code.reshape(e, t // tb, 1, tb), x1, final_w, ye)


def _pick(n, prefs):
    for p in prefs:
        if n % p == 0:
            return p
    raise ValueError(f"no tile size for {n}")


def _layer(x, lw, final_w, final_norm):
    batch, seq, d = x.shape
    t = batch * seq
    heads = lw["dt_bias_f"].shape[-1]
    d_inner = lw["ssm_norm_w"].shape[-1]
    conv_ch = lw["conv_w"].shape[-1]
    n_experts = lw["w_router"].shape[-1]
    cap = CAPACITY_FACTOR * t // n_experts
    assert seq % CHUNK == 0 and cap % LANES == 0 and 2 * heads <= LANES

    x2d = x.reshape(t, d)
    main, dt = _in_proj(x2d, lw["norm_mix_w"], lw["w_main"], lw["w_dt"], lw["dt_bias_all"],
                        _pick(t, (512, 256, 128)), _pick(lw["w_main"].shape[1], (1024, 512)))
    xc = _conv(main, lw["conv_w"], lw["conv_b"], seq, d_inner,
               _pick(seq, (512, 256, 128)), _pick(conv_ch, (1024, 512)))
    yf = _ssd(xc, dt, lw["a_log_all"], lw["d_skip_exp"], batch, seq, d_inner, heads, reverse=False)
    yb = _ssd(xc, dt, lw["a_log_all"], lw["d_skip_exp"], batch, seq, d_inner, heads, reverse=True)
    x1, h2, probs_t = _mix(yf, yb, main, x2d, lw["ssm_norm_w"], lw["w_ssm_out"], lw["w_pool"],
                           lw["pool_scale"], lw["w_o"], lw["norm_ffn_w"], lw["w_router_pad"],
                           seq, d_inner, conv_ch, n_experts, _pick(seq, (512, 256, 128)))
    idx, aff, code, off, tot = _route(probs_t, cap)
    xe = _gather(h2, idx.reshape(n_experts * cap), _pick(cap, (512, 256, 128)))
    ye = _ffn(xe, aff.reshape(n_experts * cap // LANES, LANES), lw["w_gate"], lw["w_up"], lw["w_down"],
              cap, _pick(cap, (512, 256, 128)), _pick(lw["w_gate"].shape[-1], (256, 128)))
    r = t // LANES
    out = _combine(off.reshape(n_experts, r), tot.reshape(n_experts, r), code, x1, ye, final_w,
                   cap, final_norm)
    return out.reshape(batch, seq, d)


def _prep_layer(i, norm_mix_w, w_in, conv_w, conv_b, dt_bias_f, dt_bias_b, a_log_f, a_log_b, d_skip,
                ssm_norm_w, w_ssm_out, w_pool, pool_scale, w_o, norm_ffn_w, w_router, w_gate, w_up, w_down):
    d = w_in.shape[1]
    heads = dt_bias_f.shape[-1]
    d_inner = ssm_norm_w.shape[-1]
    conv_ch = conv_w.shape[-1]
    p = d_inner // heads
    o2 = d_inner + conv_ch
    o4 = o2 + 2 * heads
    wi = w_in[i]
    w_main = jnp.concatenate([wi[:, :o2], wi[:, o4:]], axis=1).astype(BF16)
    w_dt = jnp.pad(wi[:, o2:o4], ((0, 0), (0, LANES - 2 * heads)))
    row = lambda v: v.reshape(1, -1).astype(F32)
    padl = lambda v: jnp.pad(v.reshape(1, -1).astype(F32), ((0, 0), (0, LANES - v.size)))
    return {
        "norm_mix_w": row(norm_mix_w[i]), "w_main": w_main, "w_dt": w_dt,
        "conv_w": conv_w[i], "conv_b": row(conv_b[i]),
        "dt_bias_f": dt_bias_f[i],
        "dt_bias_all": padl(jnp.concatenate([dt_bias_f[i], dt_bias_b[i]])),
        "a_log_all": padl(jnp.concatenate([a_log_f[i], a_log_b[i]])),
        "d_skip_exp": row(jnp.repeat(d_skip[i], p)),
        "ssm_norm_w": row(ssm_norm_w[i]), "w_ssm_out": w_ssm_out[i].astype(BF16),
        "w_pool": w_pool[i].astype(BF16), "pool_scale": row(pool_scale[i]),
        "w_o": w_o[i].astype(BF16), "norm_ffn_w": row(norm_ffn_w[i]),
        "w_router": w_router[i],
        "w_router_pad": jnp.pad(w_router[i].astype(F32), ((0, 0), (0, LANES - w_router.shape[-1]))),
        "w_gate": w_gate[i].astype(BF16), "w_up": w_up[i].astype(BF16), "w_down": w_down[i].astype(BF16),
    }


def kernel(x_prompt, x_sample, norm_mix_w, w_in, conv_w, conv_b, dt_bias_f, dt_bias_b, a_log_f, a_log_b,
           d_skip, ssm_norm_w, w_ssm_out, w_pool, pool_scale, w_o, norm_ffn_w, w_router, w_gate, w_up,
           w_down, norm_final_w):
    depth = w_in.shape[0]
    layers = [_prep_layer(i, norm_mix_w, w_in, conv_w, conv_b, dt_bias_f, dt_bias_b, a_log_f, a_log_b,
                          d_skip, ssm_norm_w, w_ssm_out, w_pool, pool_scale, w_o, norm_ffn_w, w_router,
                          w_gate, w_up, w_down) for i in range(depth)]
    final_w = norm_final_w.reshape(1, -1).astype(F32)

    def trunk(x):
        for i, lw in enumerate(layers):
            x = _layer(x, lw, final_w, final_norm=(i == depth - 1))
        return x

    return (trunk(x_prompt), trunk(x_sample))
```

```python
import functools

import jax
import jax.numpy as jnp
from jax import lax
from jax.experimental import pallas as pl
from jax.experimental.pallas import tpu as pltpu

F32 = jnp.float32
BF16 = jnp.bfloat16
I32 = jnp.int32

EPS = 1e-6
CHUNK = 128
LANES = 128
SUBLANES = 8
BF16_ROWS = 16
SSM_GROUPS = 8
POOL_WINDOWS = (2, 4, 8, 16)
CAPACITY_FACTOR = 2
VMEM_LIMIT = 56 * 1024 * 1024


def _cparams(sem):
    return pltpu.CompilerParams(dimension_semantics=sem, vmem_limit_bytes=VMEM_LIMIT)


def _iota(shape, dim):
    return lax.broadcasted_iota(I32, shape, dim)


def _sigmoid(x):
    return 1.0 / (1.0 + jnp.exp(-x))


def _split_bf16(x, terms):
    parts = []
    rem = x
    for _ in range(terms):
        p = rem.astype(BF16)
        parts.append(p)
        rem = rem - p.astype(F32)
    return parts


def _dot(a, b):
    return jnp.dot(a, b, preferred_element_type=F32)


def _dot_nt(a, b):
    return lax.dot_general(a, b, (((1,), (1,)), ((), ())), preferred_element_type=F32)


def _dot_exact_lhs(mask_bf, x, terms=3):
    out = None
    for p in _split_bf16(x, terms):
        d = _dot(mask_bf, p)
        out = d if out is None else out + d
    return out


def _in_proj_kernel(x_ref, nw_ref, w_ref, wdt_ref, bias_ref, alog_ref, main_ref, acum_ref, src_ref, wrow_ref,
                    h_ref, *, tn, heads):
    x = x_ref[...]
    h = x * lax.rsqrt(jnp.mean(x * x, axis=-1, keepdims=True) + EPS) * nw_ref[...]
    hb, hl = _split_bf16(h, 2)
    h_ref[...] = hb
    wb, wl = _split_bf16(wdt_ref[...], 2)
    v = _dot(hb, wb) + _dot(hl, wb) + _dot(hb, wl) + bias_ref[...]
    dt = jnp.maximum(v, 0.0) + jnp.log1p(jnp.exp(-jnp.abs(v)))

    L = CHUNK
    neg_a = -jnp.exp(alog_ref[...])
    row = _iota((L, L), 0)
    col = _iota((L, L), 1)
    tril = jnp.where(col <= row, 1.0, 0.0).astype(BF16)
    triu = jnp.where(col >= row, 1.0, 0.0).astype(BF16)
    fwd_col = col < heads
    fwd_row = _iota((LANES, 1), 0) < heads
    for k in range(x_ref.shape[0] // L):
        rows = slice(k * L, (k + 1) * L)
        dtk = dt[rows, :]
        da = dtk * neg_a
        acum = jnp.where(fwd_col, _dot_exact_lhs(tril, da), _dot_exact_lhs(triu, da))
        acum_t = acum.T
        a_last_t = jnp.where(fwd_row, acum_t[:, L - 1:L], acum_t[:, 0:1])
        src_t = acum_t - jnp.log(dtk).T
        acum_ref[rows, :] = acum
        src_ref[rows, :] = src_t
        wrow_ref[rows, :] = jnp.exp(a_last_t - src_t)

    for j in range(w_ref.shape[1] // tn):
        main_ref[:, j * tn:(j + 1) * tn] = _dot(h_ref[...], w_ref[:, j * tn:(j + 1) * tn]).astype(main_ref.dtype)


def _in_proj(x2d, norm_w, w_main, w_dt, dt_bias, alog_all, heads, tm, tn):
    t, d = x2d.shape
    ncols = w_main.shape[1]
    const = lambda shape: pl.BlockSpec(shape, lambda i: (0, 0), pipeline_mode=pl.Buffered(1))
    table = pl.BlockSpec((tm, LANES), lambda i: (i, 0))
    return pl.pallas_call(
        functools.partial(_in_proj_kernel, tn=tn, heads=heads),
        grid=(t // tm,),
        in_specs=[
            pl.BlockSpec((tm, d), lambda i: (i, 0)),
            const((1, d)),
            const((d, ncols)),
            const((d, LANES)),
            const((1, LANES)),
            const((1, LANES)),
        ],
        out_specs=[pl.BlockSpec((tm, ncols), lambda i: (i, 0)), table, table, table],
        out_shape=[jax.ShapeDtypeStruct((t, ncols), BF16)] + [jax.ShapeDtypeStruct((t, LANES), F32)] * 3,
        scratch_shapes=[pltpu.VMEM((tm, d), BF16)],
        compiler_params=_cparams(("parallel",)),
        name="in_proj",
    )(x2d, norm_w, w_main, w_dt, dt_bias, alog_all)


def _conv_kernel(cur_ref, prev_ref, next_ref, w_ref, b_ref, out_ref, buf_ref, *, tiles_per_seq):
    i = pl.program_id(0)
    tr = cur_ref.shape[0]
    halo = prev_ref.shape[0]
    pos = i % tiles_per_seq
    keep_prev = jnp.where(pos == 0, 0.0, 1.0)
    keep_next = jnp.where(pos == tiles_per_seq - 1, 0.0, 1.0)
    buf_ref[0:halo, :] = (prev_ref[...].astype(F32) * keep_prev).astype(BF16)
    buf_ref[halo:halo + tr, :] = cur_ref[...]
    buf_ref[halo + tr:halo + tr + halo, :] = (next_ref[...].astype(F32) * keep_next).astype(BF16)
    k = w_ref.shape[0]
    pad = k // 2
    win = CHUNK + 2 * halo
    ti = _iota((CHUNK, win), 0)
    ji = _iota((CHUNK, win), 1)
    shift = {j: jnp.where(ji == ti + halo + (j - pad), 1.0, 0.0).astype(BF16) for j in range(k) if j != pad}
    for b in range(tr // CHUNK):
        xw = buf_ref[b * CHUNK:b * CHUNK + win, :]
        acc = b_ref[...] + w_ref[pad:pad + 1, :] * cur_ref[b * CHUNK:(b + 1) * CHUNK, :].astype(F32)
        for j in shift:
            acc = acc + w_ref[j:j + 1, :] * _dot(shift[j], xw)
        out_ref[b * CHUNK:(b + 1) * CHUNK, :] = (acc * _sigmoid(acc)).astype(out_ref.dtype)


def _conv(main, conv_w, conv_b, seq, col0, tr, tc):
    t = main.shape[0]
    k, ch = conv_w.shape
    halo = BF16_ROWS
    cb0 = col0 // tc
    nhalo = t // halo
    rh = tr // halo
    return pl.pallas_call(
        functools.partial(_conv_kernel, tiles_per_seq=seq // tr),
        grid=(t // tr, ch // tc),
        in_specs=[
            pl.BlockSpec((tr, tc), lambda i, j: (i, cb0 + j)),
            pl.BlockSpec((halo, tc), lambda i, j: (jnp.maximum(i * rh - 1, 0), cb0 + j)),
            pl.BlockSpec((halo, tc), lambda i, j: (jnp.minimum((i + 1) * rh, nhalo - 1), cb0 + j)),
            pl.BlockSpec((k, tc), lambda i, j: (0, j)),
            pl.BlockSpec((1, tc), lambda i, j: (0, j)),
        ],
        out_specs=pl.BlockSpec((tr, tc), lambda i, j: (i, j)),
        out_shape=jax.ShapeDtypeStruct((t, ch), BF16),
        scratch_shapes=[pltpu.VMEM((tr + 2 * halo, tc), BF16)],
        compiler_params=_cparams(("parallel", "parallel")),
        name="conv_silu",
    )(main, main, main, conv_w, conv_b)


def _ssd_kernel(xs_ref, b_ref, c_ref, acum_ref, src_ref, wrow_ref, dskip_ref, y_ref, state_ref,
                *, reverse, heads, add_skip):
    @pl.when(pl.program_id(1) == 0)
    def _():
        state_ref[...] = jnp.zeros_like(state_ref)

    L = xs_ref.shape[0]
    d_inner = xs_ref.shape[1]
    p = d_inner // heads
    hpg = heads // SSM_GROUPS
    n = b_ref.shape[1] // SSM_GROUPS
    assert n == L
    assert hpg % 2 == 0
    off = heads if reverse else 0
    first = _iota((L, 2 * p), 1) < p

    row = _iota((L, L), 0)
    col = _iota((L, L), 1)
    mask = (col >= row) if reverse else (col <= row)
    acum = acum_ref[...]
    src_t = src_ref[...]
    wrow_t = wrow_ref[...]
    last = 0 if reverse else L - 1
    dchunk = jnp.exp(acum[last:last + 1, :])

    for g in range(SSM_GROUPS):
        bg = b_ref[:, g * n:(g + 1) * n]
        cg = c_ref[:, g * n:(g + 1) * n]
        cgf = cg.astype(F32)
        cb = _dot_nt(cg, bg)
        btf = bg.astype(F32).T
        for j in range(0, hpg, 2):
            hh0 = g * hpg + j
            cols = slice(hh0 * p, (hh0 + 2) * p)
            xpair = xs_ref[:, cols]
            st = state_ref[hh0 // 2]
            rhs = jnp.concatenate([xpair, st.astype(BF16)], axis=0)
            ys, ds, dch = [], [], []
            for hh in (hh0, hh0 + 1):
                h = off + hh
                colb = jnp.broadcast_to(acum[:, h:h + 1], (L, L))
                m = (cb * jnp.exp(jnp.where(mask, colb - src_t[h:h + 1, :], -1e30))).astype(BF16)
                cs = (cgf * jnp.exp(colb)).astype(BF16)
                ys.append(_dot(jnp.concatenate([m, cs], axis=1), rhs))
                bts = (btf * wrow_t[h:h + 1, :]).astype(BF16)
                ds.append(_dot(bts, xpair))
                dch.append(dchunk[:, h:h + 1])
            y = jnp.where(first, ys[0], ys[1])
            state_ref[hh0 // 2] = (st * jnp.where(first[0:1, :], dch[0], dch[1])
                                   + jnp.where(first, ds[0], ds[1]))
            if add_skip:
                y = y + xpair.astype(F32) * dskip_ref[:, cols]
            y_ref[:, cols] = y.astype(y_ref.dtype)


def _ssd(xc, acum, src_t, wrow_t, dskip_exp, batch, seq, d_inner, heads, reverse):
    t = xc.shape[0]
    nc = seq // CHUNK
    gn = (xc.shape[1] - d_inner) // 2
    bblk = d_inner // gn
    n = gn // SSM_GROUPS

    def rowblk(b, c):
        return b * nc + (nc - 1 - c if reverse else c)

    return pl.pallas_call(
        functools.partial(_ssd_kernel, reverse=reverse, heads=heads, add_skip=not reverse),
        grid=(batch, nc),
        in_specs=[
            pl.BlockSpec((CHUNK, d_inner), lambda b, c: (rowblk(b, c), 0)),
            pl.BlockSpec((CHUNK, gn), lambda b, c: (rowblk(b, c), bblk)),
            pl.BlockSpec((CHUNK, gn), lambda b, c: (rowblk(b, c), bblk + 1)),
            pl.BlockSpec((CHUNK, LANES), lambda b, c: (rowblk(b, c), 0)),
            pl.BlockSpec((CHUNK, LANES), lambda b, c: (rowblk(b, c), 0)),
            pl.BlockSpec((CHUNK, LANES), lambda b, c: (rowblk(b, c), 0)),
            pl.BlockSpec((1, d_inner), lambda b, c: (0, 0)),
        ],
        out_specs=pl.BlockSpec((CHUNK, d_inner), lambda b, c: (rowblk(b, c), 0)),
        out_shape=jax.ShapeDtypeStruct((t, d_inner), BF16),
        scratch_shapes=[pltpu.VMEM((heads // 2, n, 2 * d_inner // heads), F32)],
        compiler_params=_cparams(("parallel", "arbitrary")),
        name="ssd_bwd" if reverse else "ssd_fwd",
    )(xc, xc, xc, acum, src_t, wrow_t, dskip_exp)


def _mix_kernel(yf_ref, yb_ref, z_ref, u_ref, up_ref, un_ref, ga_ref, gb_ref, x_ref,
                snw_ref, wso_ref, wp_ref, ps_ref, wo_ref, fnw_ref, wr_ref,
                x1_ref, h2_ref, pt_ref, *, tiles_per_seq, seq, n_experts):
    tm = x_ref.shape[0]
    d_inner = z_ref.shape[1]
    gw = d_inner // SSM_GROUPS
    halo = up_ref.shape[0]
    sub = min(tm, 2 * CHUNK)
    pos0 = (pl.program_id(0) % tiles_per_seq) * tm
    pw = u_ref.shape[1] // len(POOL_WINDOWS)
    keep_prev = jnp.where(pos0 > 0, 1.0, 0.0)
    keep_next = jnp.where(pos0 + tm < seq, 1.0, 0.0)
    wb, wl = _split_bf16(wr_ref[...], 2)

    tq = _iota((sub, sub), 0)
    sq = _iota((sub, sub), 1)
    tp = _iota((sub, halo), 0)
    jp = _iota((sub, halo), 1)
    bands = []
    for win in POOL_WINDOWS:
        lo = win // 2
        hi = win - lo
        in_c = (sq >= tq - lo) & (sq < tq + hi)
        in_p = jp - halo >= tp - lo
        in_n = jp + sub < tp + hi
        bands.append((jnp.where(in_c, 1.0, 0.0).astype(BF16), in_p, in_n))

    for si in range(tm // sub):
        rows = slice(si * sub, (si + 1) * sub)
        at_top = si == 0
        at_bottom = si == tm // sub - 1

        z = z_ref[rows, :].astype(F32)
        y = (yf_ref[rows, :].astype(F32) + yb_ref[rows, :].astype(F32)) * (z * _sigmoid(z))
        parts = []
        for g in range(SSM_GROUPS):
            blk = y[:, g * gw:(g + 1) * gw]
            sc = lax.rsqrt(jnp.mean(blk * blk, axis=-1, keepdims=True) + EPS)
            parts.append((blk * sc * snw_ref[:, g * gw:(g + 1) * gw]).astype(BF16))
        y_a = _dot(jnp.concatenate(parts, axis=1), wso_ref[...])

        tpos = pos0 + si * sub + _iota((sub, 1), 0)
        pooled = []
        for gi, win in enumerate(POOL_WINDOWS):
            lo = win // 2
            hi = win - lo
            band_c, in_p, in_n = bands[gi]
            cg = slice(gi * pw, (gi + 1) * pw)
            ug = u_ref[rows, cg]
            u_prev = up_ref[:, cg] if at_top else u_ref[si * sub - halo:si * sub, cg]
            u_next = un_ref[:, cg] if at_bottom else u_ref[(si + 1) * sub:(si + 1) * sub + halo, cg]
            band_p = jnp.where(in_p, keep_prev if at_top else 1.0, 0.0).astype(BF16)
            band_n = jnp.where(in_n, keep_next if at_bottom else 1.0, 0.0).astype(BF16)
            s = _dot(band_c, ug) + _dot(band_p, u_prev) + _dot(band_n, u_next)
            cnt = (jnp.minimum(tpos + hi, seq) - jnp.maximum(tpos - lo, 0)).astype(F32)
            pg = (s / cnt - ug.astype(F32)).astype(BF16)
            pooled.append(_dot(pg, wp_ref[gi]))
        y_b = jnp.concatenate(pooled, axis=1) * ps_ref[...]

        merged = (_sigmoid(ga_ref[rows, :].astype(F32)) * y_a
                  + _sigmoid(gb_ref[rows, :].astype(F32)) * y_b)
        x1 = x_ref[rows, :] + _dot(merged.astype(BF16), wo_ref[...])
        x1_ref[rows, :] = x1

        h2 = x1 * lax.rsqrt(jnp.mean(x1 * x1, axis=-1, keepdims=True) + EPS) * fnw_ref[...]
        h2_ref[rows, :] = h2
        hb, hl = _split_bf16(h2, 2)
        logits = _dot(hb, wb) + _dot(hl, wb) + _dot(hb, wl)
        logits = jnp.where(_iota(logits.shape, 1) < n_experts, logits, -1e30)
        e = jnp.exp(logits - jnp.max(logits, axis=-1, keepdims=True))
        probs = e / jnp.sum(e, axis=-1, keepdims=True)
        pt_ref[:, rows] = probs.T[:n_experts, :]


def _mix(yf, yb, main, x2d, ssm_norm_w, w_ssm_out, w_pool, pool_scale, w_o, norm_ffn_w, w_router_pad,
         seq, d_inner, conv_ch, n_experts, tm):
    t, d = x2d.shape
    halo = BF16_ROWS
    rh = tm // halo
    nhalo = t // halo
    zb = 0
    ub = (d_inner + conv_ch) // d
    full = lambda shape: pl.BlockSpec(shape, lambda i: (0,) * len(shape))
    return pl.pallas_call(
        functools.partial(_mix_kernel, tiles_per_seq=seq // tm, seq=seq, n_experts=n_experts),
        grid=(t // tm,),
        in_specs=[
            pl.BlockSpec((tm, d_inner), lambda i: (i, 0)),
            pl.BlockSpec((tm, d_inner), lambda i: (i, 0)),
            pl.BlockSpec((tm, d_inner), lambda i: (i, zb)),
            pl.BlockSpec((tm, d), lambda i: (i, ub)),
            pl.BlockSpec((halo, d), lambda i: (jnp.maximum(i * rh - 1, 0), ub)),
            pl.BlockSpec((halo, d), lambda i: (jnp.minimum((i + 1) * rh, nhalo - 1), ub)),
            pl.BlockSpec((tm, d), lambda i: (i, ub + 1)),
            pl.BlockSpec((tm, d), lambda i: (i, ub + 2)),
            pl.BlockSpec((tm, d), lambda i: (i, 0)),
            full((1, d_inner)),
            full((d_inner, d)),
            full(w_pool.shape),
            full((1, d)),
            full((d, d)),
            full((1, d)),
            full((d, LANES)),
        ],
        out_specs=[
            pl.BlockSpec((tm, d), lambda i: (i, 0)),
            pl.BlockSpec((tm, d), lambda i: (i, 0)),
            pl.BlockSpec((n_experts, tm), lambda i: (0, i)),
        ],
        out_shape=[
            jax.ShapeDtypeStruct((t, d), F32),
            jax.ShapeDtypeStruct((t, d), F32),
            jax.ShapeDtypeStruct((n_experts, t), F32),
        ],
        compiler_params=_cparams(("parallel",)),
        name="mix",
    )(yf, yb, main, main, main, main, main, main, x2d,
      ssm_norm_w, w_ssm_out, w_pool, pool_scale, w_o, norm_ffn_w, w_router_pad)


def _route_kernel(p_ref, idx_ref, aff_ref, code_ref, off_ref, tot_ref, *, cap, ct):
    p = p_ref[0]
    r = p.shape[0]
    key = pltpu.bitcast(p, I32)

    def search(i, prefix):
        cand = prefix | lax.shift_left(jnp.int32(1), 30 - i)
        cnt = jnp.sum(jnp.where(key >= cand, 1, 0))
        return jnp.where(cnt >= cap, cand, prefix)

    tau = lax.fori_loop(0, 31, search, jnp.int32(0))

    lane_r = _iota((LANES, LANES), 0)
    lane_c = _iota((LANES, LANES), 1)
    tri_incl = jnp.where(lane_r <= lane_c, 1.0, 0.0).astype(BF16)
    rr = _iota((r, r), 0)
    rc = _iota((r, r), 1)
    tril_strict = jnp.where(rc < rr, 1.0, 0.0).astype(BF16)

    def cumsum(mask):
        mb = jnp.where(mask, 1.0, 0.0).astype(BF16)
        incl = _dot(mb, tri_incl)
        tot = jnp.broadcast_to(incl[:, LANES - 1:LANES], (r, LANES))
        off = _dot(tril_strict, tot.astype(BF16))
        return mb, incl, off, tot

    gt = key > tau
    eq = key == tau
    need = (cap - jnp.sum(jnp.where(gt, 1, 0))).astype(F32)
    mb_eq, incl_eq, off_eq, _ = cumsum(eq)
    rank_eq = off_eq + incl_eq - mb_eq.astype(F32)
    sel = gt | (eq & (rank_eq < need))
    mb, incl, off, tot = cumsum(sel)

    local = incl - mb.astype(F32)
    code_ref[0] = jnp.where(sel, local, -1.0).astype(I32)

    ones8 = jnp.ones((SUBLANES, LANES), BF16)
    tot_t = _dot_nt(ones8, mb)
    triu_strict = jnp.where(rr < rc, 1.0, 0.0).astype(BF16)
    off_t = _dot(tot_t.astype(BF16), triu_strict)
    off_ref[0] = off_t[0:1, :].astype(I32)
    tot_ref[0] = tot_t[0:1, :].astype(I32)

    incl_t = _dot_nt(jnp.where(lane_c <= lane_r, 1.0, 0.0).astype(BF16), mb).astype(BF16)
    p_t = _split_bf16(p.T, 3)
    incl_row = (off + tot)[:, 0:1]
    off_col = off[:, 0:1]
    for c0 in range(0, cap, ct):
        cvec = (c0 + _iota((1, ct), 1)).astype(F32)
        before = incl_row <= cvec
        rsel = jnp.sum(jnp.where(before, 1, 0), axis=0, keepdims=True)
        onehot = _iota((r, ct), 0) == rsel
        onehot_bf = jnp.where(onehot, 1.0, 0.0).astype(BF16)
        g_t = _dot(incl_t, onehot_bf)
        offc = jnp.sum(jnp.where(onehot, off_col, 0.0), axis=0, keepdims=True)
        target = cvec - offc
        lane = jnp.sum(jnp.where(g_t <= target, 1, 0), axis=0, keepdims=True)
        idx_ref[0, :, c0:c0 + ct] = rsel * LANES + lane
        a_t = _dot(p_t[0], onehot_bf) + _dot(p_t[1], onehot_bf) + _dot(p_t[2], onehot_bf)
        hit = _iota((LANES, ct), 0) == lane
        aff_ref[0, :, c0:c0 + ct] = jnp.sum(jnp.where(hit, a_t, 0.0), axis=0, keepdims=True)


def _route(probs_t, cap):
    e, t = probs_t.shape
    r = t // LANES
    ct = min(cap, 512)
    p3 = probs_t.reshape(e, r, LANES)
    return pl.pallas_call(
        functools.partial(_route_kernel, cap=cap, ct=ct),
        grid=(e,),
        in_specs=[pl.BlockSpec((1, r, LANES), lambda i: (i, 0, 0))],
        out_specs=[
            pl.BlockSpec((1, 1, cap), lambda i: (i, 0, 0)),
            pl.BlockSpec((1, 1, cap), lambda i: (i, 0, 0)),
            pl.BlockSpec((1, r, LANES), lambda i: (i, 0, 0)),
            pl.BlockSpec((1, 1, r), lambda i: (i, 0, 0)),
            pl.BlockSpec((1, 1, r), lambda i: (i, 0, 0)),
        ],
        out_shape=[
            jax.ShapeDtypeStruct((e, 1, cap), I32),
            jax.ShapeDtypeStruct((e, 1, cap), F32),
            jax.ShapeDtypeStruct((e, r, LANES), I32),
            jax.ShapeDtypeStruct((e, 1, r), I32),
            jax.ShapeDtypeStruct((e, 1, r), I32),
        ],
        compiler_params=_cparams(("parallel",)),
        name="route",
    )(p3)


def _gather_kernel(idx_ref, h_hbm, out_ref, sem):
    rows = out_ref.shape[0]

    def issue(i, carry):
        tok = idx_ref[0, 0, i]
        pltpu.make_async_copy(h_hbm.at[pl.ds(tok, 1), :], out_ref.at[pl.ds(i, 1), :], sem).start()
        return carry

    lax.fori_loop(0, rows, issue, 0)
    pltpu.make_async_copy(h_hbm.at[pl.ds(0, rows), :], out_ref, sem).wait()


def _gather(h2, idx_flat, tg):
    t, d = h2.shape
    nrows = idx_flat.shape[0]
    idx3 = idx_flat.reshape(nrows // tg, 1, tg)
    return pl.pallas_call(
        _gather_kernel,
        grid=(nrows // tg,),
        in_specs=[
            pl.BlockSpec((1, 1, tg), lambda i: (i, 0, 0), memory_space=pltpu.SMEM),
            pl.BlockSpec(memory_space=pl.ANY),
        ],
        out_specs=pl.BlockSpec((tg, d), lambda i: (i, 0)),
        out_shape=jax.ShapeDtypeStruct((nrows, d), h2.dtype),
        scratch_shapes=[pltpu.SemaphoreType.DMA],
        compiler_params=_cparams(("arbitrary",)),
        name="gather",
    )(idx3, h2)


def _ffn_kernel(xe_ref, aff_ref, wg_ref, wu_ref, wd_ref, out_ref, xb_ref, hid_ref, *, fc):
    xb_ref[...] = xe_ref[...].astype(BF16)
    for c in range(wg_ref.shape[1] // fc):
        g = _dot(xb_ref[...], wg_ref[:, c * fc:(c + 1) * fc])
        u = _dot(xb_ref[...], wu_ref[:, c * fc:(c + 1) * fc])
        hid_ref[:, c * fc:(c + 1) * fc] = (g * _sigmoid(g) * u).astype(BF16)
    y = _dot(hid_ref[...], wd_ref[...])
    for j in range(aff_ref.shape[0]):
        a = jnp.broadcast_to(aff_ref[j:j + 1, :], (LANES, LANES)).T[:, 0:1]
        out_ref[j * LANES:(j + 1) * LANES, :] = (y[j * LANES:(j + 1) * LANES, :] * a).astype(out_ref.dtype)


def _ffn(xe, aff2d, w_gate, w_up, w_down, cap, rs, fc):
    nrows, d = xe.shape
    e, _, ff = w_gate.shape
    rpe = cap // rs
    return pl.pallas_call(
        functools.partial(_ffn_kernel, fc=fc),
        grid=(e, rpe),
        in_specs=[
            pl.BlockSpec((rs, d), lambda ei, r: (ei * rpe + r, 0)),
            pl.BlockSpec((None, rs // LANES, LANES), lambda ei, r: (ei * rpe + r, 0, 0)),
            pl.BlockSpec((None, d, ff), lambda ei, r: (ei, 0, 0)),
            pl.BlockSpec((None, d, ff), lambda ei, r: (ei, 0, 0)),
            pl.BlockSpec((None, ff, d), lambda ei, r: (ei, 0, 0)),
        ],
        out_specs=pl.BlockSpec((rs, d), lambda ei, r: (ei * rpe + r, 0)),
        out_shape=jax.ShapeDtypeStruct((nrows, d), BF16),
        scratch_shapes=[pltpu.VMEM((rs, d), BF16), pltpu.VMEM((rs, ff), BF16)],
        compiler_params=_cparams(("parallel", "arbitrary")),
        name="expert_ffn",
    )(xe, aff2d.reshape(nrows // rs, rs // LANES, LANES), w_gate, w_up, w_down)


def _combine_kernel(off_ref, tot_ref, code_ref, x1_ref, fw_ref, ye_hbm, out_ref, buf_ref, sem,
                    *, cap, n_experts, qc, final_norm):
    blk = pl.program_id(0)
    nblk = pl.num_programs(0)
    tb = x1_ref.shape[0]

    ch = BF16_ROWS

    def layout(b):
        qbase = jnp.int32(0)
        chunks, shifts = [], []
        for e in range(n_experts):
            s = off_ref[e, b]
            n = tot_ref[e, b]
            c_lo = s // ch
            nch = jnp.where(n > 0, (s + n - 1) // ch - c_lo + 1, 0)
            chunks.append((c_lo, nch, qbase))
            shifts.append(qbase + (s - c_lo * ch))
            qbase = qbase + nch * ch
        return chunks, shifts, qbase

    def chunk_copy(src, dst, slot):
        return pltpu.make_async_copy(ye_hbm.at[pl.ds(src, ch), :],
                                     buf_ref.at[slot, pl.ds(dst, ch), :], sem.at[slot])

    def fetch(b, slot):
        chunks, _, _ = layout(b)
        for e, (c_lo, nch, qbase) in enumerate(chunks):
            def issue(i, carry, e=e, c_lo=c_lo, qbase=qbase):
                src = pl.multiple_of(e * cap + (c_lo + i) * ch, ch)
                dst = pl.multiple_of(qbase + i * ch, ch)
                chunk_copy(src, dst, slot).start()
                return carry

            lax.fori_loop(0, nch, issue, 0)

    slot = blk % 2

    @pl.when(blk == 0)
    def _():
        buf_ref[...] = jnp.zeros_like(buf_ref)
        fetch(blk, slot)

    @pl.when(blk + 1 < nblk)
    def _():
        fetch(blk + 1, 1 - slot)

    _, shifts, qtot = layout(blk)

    def drain(i, carry):
        chunk_copy(0, 0, slot).wait()
        return carry

    lax.fori_loop(0, qtot // ch, drain, 0)

    code_t = code_ref[:, 0, :].astype(F32).T
    nq = (qtot + qc - 1) // qc
    tgt = []
    for e in range(n_experts):
        ce = code_t[:, e:e + 1].astype(I32)
        tgt.append(jnp.broadcast_to(jnp.where(ce >= 0, ce + shifts[e], -1), (tb, LANES)))
    lane = _iota((tb, LANES), 1)

    def seg_sum(k, acc):
        q0 = pl.multiple_of(k * qc, qc)
        cols = []
        for c in range(qc // LANES):
            qi = q0 + c * LANES + lane
            s_col = jnp.zeros((tb, LANES), F32)
            for e in range(n_experts):
                s_col = jnp.where(tgt[e] == qi, 1.0, s_col)
            cols.append(s_col.astype(BF16))
        return acc + _dot(jnp.concatenate(cols, axis=1), buf_ref[slot, pl.ds(q0, qc), :])

    ffn = lax.fori_loop(0, nq, seg_sum, jnp.zeros(x1_ref.shape, F32))
    x2 = x1_ref[...] + ffn
    if final_norm:
        x2 = x2 * lax.rsqrt(jnp.mean(x2 * x2, axis=-1, keepdims=True) + EPS) * fw_ref[...]
    out_ref[...] = x2


def _combine(off, tot, code, x1, ye, final_w, cap, final_norm):
    t, d = x1.shape
    e = code.shape[0]
    tb = LANES
    qc = 256
    max_rows = e * (tb + 2 * BF16_ROWS)
    buf_rows = ((max_rows + qc - 1) // qc) * qc
    return pl.pallas_call(
        functools.partial(_combine_kernel, cap=cap, n_experts=e, qc=qc, final_norm=final_norm),
        grid_spec=pltpu.PrefetchScalarGridSpec(
            num_scalar_prefetch=2,
            grid=(t // tb,),
            in_specs=[
                pl.BlockSpec((e, None, 1, tb), lambda i, *_: (0, i, 0, 0)),
                pl.BlockSpec((tb, d), lambda i, *_: (i, 0)),
                pl.BlockSpec((1, d), lambda i, *_: (0, 0)),
                pl.BlockSpec(memory_space=pl.ANY),
            ],
            out_specs=pl.BlockSpec((tb, d), lambda i, *_: (i, 0)),
            scratch_shapes=[pltpu.VMEM((2, buf_rows, d), ye.dtype), pltpu.SemaphoreType.DMA((2,))],
        ),
        out_shape=jax.ShapeDtypeStruct((t, d), F32),
        compiler_params=_cparams(("arbitrary",)),
        name="combine",
    )(off, tot, code.reshape(e, t // tb, 1, tb), x1, final_w, ye)


def _pick(n, prefs):
    for p in prefs:
        if n % p == 0:
            return p
    raise ValueError(f"no tile size for {n}")


def _layer(x, lw, final_w, final_norm):
    batch, seq, d = x.shape
    t = batch * seq
    heads = lw["dt_bias_f"].shape[-1]
    d_inner = lw["ssm_norm_w"].shape[-1]
    conv_ch = lw["conv_w"].shape[-1]
    n_experts = lw["w_router"].shape[-1]
    cap = CAPACITY_FACTOR * t // n_experts
    assert seq % CHUNK == 0 and cap % LANES == 0 and 2 * heads <= LANES and CHUNK == LANES

    x2d = x.reshape(t, d)
    main, acum, src_t, wrow_t = _in_proj(
        x2d, lw["norm_mix_w"], lw["w_main"], lw["w_dt"], lw["dt_bias_all"], lw["a_log_all"], heads,
        _pick(t, (512, 256, 128)), _pick(lw["w_main"].shape[1], (1024, 512)))
    xc = _conv(main, lw["conv_w"], lw["conv_b"], seq, d_inner,
               _pick(seq, (512, 256, 128)), _pick(conv_ch, (1024, 512)))
    yf = _ssd(xc, acum, src_t, wrow_t, lw["d_skip_exp"], batch, seq, d_inner, heads, reverse=False)
    yb = _ssd(xc, acum, src_t, wrow_t, lw["d_skip_exp"], batch, seq, d_inner, heads, reverse=True)
    x1, h2, probs_t = _mix(yf, yb, main, x2d, lw["ssm_norm_w"], lw["w_ssm_out"], lw["w_pool"],
                           lw["pool_scale"], lw["w_o"], lw["norm_ffn_w"], lw["w_router_pad"],
                           seq, d_inner, conv_ch, n_experts, _pick(seq, (512, 256, 128)))
    idx, aff, code, off, tot = _route(probs_t, cap)
    xe = _gather(h2, idx.reshape(n_experts * cap), _pick(cap, (512, 256, 128)))
    ye = _ffn(xe, aff.reshape(n_experts * cap // LANES, LANES), lw["w_gate"], lw["w_up"], lw["w_down"],
              cap, _pick(cap, (512, 256, 128)), _pick(lw["w_gate"].shape[-1], (256, 128)))
    r = t // LANES
    out = _combine(off.reshape(n_experts, r), tot.reshape(n_experts, r), code, x1, ye, final_w,
                   cap, final_norm)
    return out.reshape(batch, seq, d)


def _prep_layer(i, norm_mix_w, w_in, conv_w, conv_b, dt_bias_f, dt_bias_b, a_log_f, a_log_b, d_skip,
                ssm_norm_w, w_ssm_out, w_pool, pool_scale, w_o, norm_ffn_w, w_router, w_gate, w_up, w_down):
    d = w_in.shape[1]
    heads = dt_bias_f.shape[-1]
    d_inner = ssm_norm_w.shape[-1]
    conv_ch = conv_w.shape[-1]
    p = d_inner // heads
    o2 = d_inner + conv_ch
    o4 = o2 + 2 * heads
    wi = w_in[i]
    w_main = jnp.concatenate([wi[:, :o2], wi[:, o4:]], axis=1).astype(BF16)
    w_dt = jnp.pad(wi[:, o2:o4], ((0, 0), (0, LANES - 2 * heads)))
    row = lambda v: v.reshape(1, -1).astype(F32)
    padl = lambda v: jnp.pad(v.reshape(1, -1).astype(F32), ((0, 0), (0, LANES - v.size)))
    return {
        "norm_mix_w": row(norm_mix_w[i]), "w_main": w_main, "w_dt": w_dt,
        "conv_w": conv_w[i], "conv_b": row(conv_b[i]),
        "dt_bias_f": dt_bias_f[i],
        "dt_bias_all": padl(jnp.concatenate([dt_bias_f[i], dt_bias_b[i]])),
        "a_log_all": padl(jnp.concatenate([a_log_f[i], a_log_b[i]])),
        "d_skip_exp": row(jnp.repeat(d_skip[i], p)),
        "ssm_norm_w": row(ssm_norm_w[i]), "w_ssm_out": w_ssm_out[i].astype(BF16),
        "w_pool": w_pool[i].astype(BF16), "pool_scale": row(pool_scale[i]),
        "w_o": w_o[i].astype(BF16), "norm_ffn_w": row(norm_ffn_w[i]),
        "w_router": w_router[i],
        "w_router_pad": jnp.pad(w_router[i].astype(F32), ((0, 0), (0, LANES - w_router.shape[-1]))),
        "w_gate": w_gate[i].astype(BF16), "w_up": w_up[i].astype(BF16), "w_down": w_down[i].astype(BF16),
    }


def kernel(x_prompt, x_sample, norm_mix_w, w_in, conv_w, conv_b, dt_bias_f, dt_bias_b, a_log_f, a_log_b,
           d_skip, ssm_norm_w, w_ssm_out, w_pool, pool_scale, w_o, norm_ffn_w, w_router, w_gate, w_up,
           w_down, norm_final_w):
    depth = w_in.shape[0]
    layers = [_prep_layer(i, norm_mix_w, w_in, conv_w, conv_b, dt_bias_f, dt_bias_b, a_log_f, a_log_b,
                          d_skip, ssm_norm_w, w_ssm_out, w_pool, pool_scale, w_o, norm_ffn_w, w_router,
                          w_gate, w_up, w_down) for i in range(depth)]
    final_w = norm_final_w.reshape(1, -1).astype(F32)

    def trunk(x):
        for i, lw in enumerate(layers):
            x = _layer(x, lw, final_w, final_norm=(i == depth - 1))
        return x

    return (trunk(x_prompt), trunk(x_sample))
```

```python
import functools

import jax
import jax.numpy as jnp
from jax import lax
from jax.experimental import pallas as pl
from jax.experimental.pallas import tpu as pltpu
from jax.experimental.pallas import tpu_sc as plsc

F32 = jnp.float32
BF16 = jnp.bfloat16
I32 = jnp.int32

EPS = 1e-6
CHUNK = 128
LANES = 128
SUBLANES = 8
BF16_ROWS = 16
SSM_GROUPS = 8
POOL_WINDOWS = (2, 4, 8, 16)
CAPACITY_FACTOR = 2
VMEM_LIMIT = 56 * 1024 * 1024


def _cparams(sem):
    return pltpu.CompilerParams(dimension_semantics=sem, vmem_limit_bytes=VMEM_LIMIT)


def _iota(shape, dim):
    return lax.broadcasted_iota(I32, shape, dim)


def _sigmoid(x):
    return 1.0 / (1.0 + jnp.exp(-x))


def _split_bf16(x, terms):
    parts = []
    rem = x
    for _ in range(terms):
        p = rem.astype(BF16)
        parts.append(p)
        rem = rem - p.astype(F32)
    return parts


def _dot(a, b):
    return jnp.dot(a, b, preferred_element_type=F32)


def _dot_nt(a, b):
    return lax.dot_general(a, b, (((1,), (1,)), ((), ())), preferred_element_type=F32)


def _dot_exact_lhs(mask_bf, x, terms=3):
    out = None
    for p in _split_bf16(x, terms):
        d = _dot(mask_bf, p)
        out = d if out is None else out + d
    return out


def _in_proj_kernel(x_ref, nw_ref, w_ref, wdt_ref, bias_ref, alog_ref, main_ref, acum_ref, src_ref, wrow_ref,
                    h_ref, *, tn, heads):
    x = x_ref[...]
    h = x * lax.rsqrt(jnp.mean(x * x, axis=-1, keepdims=True) + EPS) * nw_ref[...]
    hb, hl = _split_bf16(h, 2)
    h_ref[...] = hb
    wb, wl = _split_bf16(wdt_ref[...], 2)
    v = _dot(hb, wb) + _dot(hl, wb) + _dot(hb, wl) + bias_ref[...]
    dt = jnp.maximum(v, 0.0) + jnp.log1p(jnp.exp(-jnp.abs(v)))

    L = CHUNK
    neg_a = -jnp.exp(alog_ref[...])
    row = _iota((L, L), 0)
    col = _iota((L, L), 1)
    tril = jnp.where(col <= row, 1.0, 0.0).astype(BF16)
    triu = jnp.where(col >= row, 1.0, 0.0).astype(BF16)
    fwd_col = col < heads
    fwd_row = _iota((LANES, 1), 0) < heads
    for k in range(x_ref.shape[0] // L):
        rows = slice(k * L, (k + 1) * L)
        dtk = dt[rows, :]
        da = dtk * neg_a
        acum = jnp.where(fwd_col, _dot_exact_lhs(tril, da), _dot_exact_lhs(triu, da))
        acum_t = acum.T
        a_last_t = jnp.where(fwd_row, acum_t[:, L - 1:L], acum_t[:, 0:1])
        src_t = acum_t - jnp.log(dtk).T
        acum_ref[rows, :] = acum
        src_ref[rows, :] = src_t
        wrow_ref[rows, :] = jnp.exp(a_last_t - src_t)

    for j in range(w_ref.shape[1] // tn):
        main_ref[:, j * tn:(j + 1) * tn] = _dot(h_ref[...], w_ref[:, j * tn:(j + 1) * tn]).astype(main_ref.dtype)


def _in_proj(x2d, norm_w, w_main, w_dt, dt_bias, alog_all, heads, tm, tn):
    t, d = x2d.shape
    ncols = w_main.shape[1]
    const = lambda shape: pl.BlockSpec(shape, lambda i: (0, 0), pipeline_mode=pl.Buffered(1))
    table = pl.BlockSpec((tm, LANES), lambda i: (i, 0))
    return pl.pallas_call(
        functools.partial(_in_proj_kernel, tn=tn, heads=heads),
        grid=(t // tm,),
        in_specs=[
            pl.BlockSpec((tm, d), lambda i: (i, 0)),
            const((1, d)),
            const((d, ncols)),
            const((d, LANES)),
            const((1, LANES)),
            const((1, LANES)),
        ],
        out_specs=[pl.BlockSpec((tm, ncols), lambda i: (i, 0)), table, table, table],
        out_shape=[jax.ShapeDtypeStruct((t, ncols), BF16)] + [jax.ShapeDtypeStruct((t, LANES), F32)] * 3,
        scratch_shapes=[pltpu.VMEM((tm, d), BF16)],
        compiler_params=_cparams(("parallel",)),
        name="in_proj",
    )(x2d, norm_w, w_main, w_dt, dt_bias, alog_all)


def _conv_kernel(cur_ref, prev_ref, next_ref, w_ref, b_ref, out_ref, buf_ref, *, tiles_per_seq):
    i = pl.program_id(0)
    tr = cur_ref.shape[0]
    halo = prev_ref.shape[0]
    pos = i % tiles_per_seq
    keep_prev = jnp.where(pos == 0, 0.0, 1.0)
    keep_next = jnp.where(pos == tiles_per_seq - 1, 0.0, 1.0)
    buf_ref[0:halo, :] = (prev_ref[...].astype(F32) * keep_prev).astype(BF16)
    buf_ref[halo:halo + tr, :] = cur_ref[...]
    buf_ref[halo + tr:halo + tr + halo, :] = (next_ref[...].astype(F32) * keep_next).astype(BF16)
    k = w_ref.shape[0]
    pad = k // 2
    win = CHUNK + 2 * halo
    ti = _iota((CHUNK, win), 0)
    ji = _iota((CHUNK, win), 1)
    shift = {j: jnp.where(ji == ti + halo + (j - pad), 1.0, 0.0).astype(BF16) for j in range(k) if j != pad}
    for b in range(tr // CHUNK):
        xw = buf_ref[b * CHUNK:b * CHUNK + win, :]
        acc = b_ref[...] + w_ref[pad:pad + 1, :] * cur_ref[b * CHUNK:(b + 1) * CHUNK, :].astype(F32)
        for j in shift:
            acc = acc + w_ref[j:j + 1, :] * _dot(shift[j], xw)
        out_ref[b * CHUNK:(b + 1) * CHUNK, :] = (acc * _sigmoid(acc)).astype(out_ref.dtype)


def _conv(main, conv_w, conv_b, seq, col0, tr, tc):
    t = main.shape[0]
    k, ch = conv_w.shape
    halo = BF16_ROWS
    cb0 = col0 // tc
    nhalo = t // halo
    rh = tr // halo
    return pl.pallas_call(
        functools.partial(_conv_kernel, tiles_per_seq=seq // tr),
        grid=(t // tr, ch // tc),
        in_specs=[
            pl.BlockSpec((tr, tc), lambda i, j: (i, cb0 + j)),
            pl.BlockSpec((halo, tc), lambda i, j: (jnp.maximum(i * rh - 1, 0), cb0 + j)),
            pl.BlockSpec((halo, tc), lambda i, j: (jnp.minimum((i + 1) * rh, nhalo - 1), cb0 + j)),
            pl.BlockSpec((k, tc), lambda i, j: (0, j)),
            pl.BlockSpec((1, tc), lambda i, j: (0, j)),
        ],
        out_specs=pl.BlockSpec((tr, tc), lambda i, j: (i, j)),
        out_shape=jax.ShapeDtypeStruct((t, ch), BF16),
        scratch_shapes=[pltpu.VMEM((tr + 2 * halo, tc), BF16)],
        compiler_params=_cparams(("parallel", "parallel")),
        name="conv_silu",
    )(main, main, main, conv_w, conv_b)


def _ssd_kernel(xs_ref, b_ref, c_ref, acum_ref, src_ref, wrow_ref, dskip_ref, y_ref, state_ref,
                *, reverse, heads, add_skip):
    @pl.when(pl.program_id(1) == 0)
    def _():
        state_ref[...] = jnp.zeros_like(state_ref)

    L = xs_ref.shape[0]
    d_inner = xs_ref.shape[1]
    p = d_inner // heads
    hpg = heads // SSM_GROUPS
    n = b_ref.shape[1] // SSM_GROUPS
    assert n == L
    assert hpg % 2 == 0
    off = heads if reverse else 0
    first = _iota((L, 2 * p), 1) < p

    row = _iota((L, L), 0)
    col = _iota((L, L), 1)
    mask = (col >= row) if reverse else (col <= row)
    acum = acum_ref[...]
    src_t = src_ref[...]
    wrow_t = wrow_ref[...]
    last = 0 if reverse else L - 1
    dchunk = jnp.exp(acum[last:last + 1, :])

    for g in range(SSM_GROUPS):
        bg = b_ref[:, g * n:(g + 1) * n]
        cg = c_ref[:, g * n:(g + 1) * n]
        cgf = cg.astype(F32)
        cb = _dot_nt(cg, bg)
        btf = bg.astype(F32).T
        for j in range(0, hpg, 2):
            hh0 = g * hpg + j
            cols = slice(hh0 * p, (hh0 + 2) * p)
            xpair = xs_ref[:, cols]
            st = state_ref[hh0 // 2]
            rhs = jnp.concatenate([xpair, st.astype(BF16)], axis=0)
            ys, ds, dch = [], [], []
            for hh in (hh0, hh0 + 1):
                h = off + hh
                colb = jnp.broadcast_to(acum[:, h:h + 1], (L, L))
                m = (cb * jnp.exp(jnp.where(mask, colb - src_t[h:h + 1, :], -1e30))).astype(BF16)
                cs = (cgf * jnp.exp(colb)).astype(BF16)
                ys.append(_dot(jnp.concatenate([m, cs], axis=1), rhs))
                bts = (btf * wrow_t[h:h + 1, :]).astype(BF16)
                ds.append(_dot(bts, xpair))
                dch.append(dchunk[:, h:h + 1])
            y = jnp.where(first, ys[0], ys[1])
            state_ref[hh0 // 2] = (st * jnp.where(first[0:1, :], dch[0], dch[1])
                                   + jnp.where(first, ds[0], ds[1]))
            if add_skip:
                y = y + xpair.astype(F32) * dskip_ref[:, cols]
            y_ref[:, cols] = y.astype(y_ref.dtype)


def _ssd(xc, acum, src_t, wrow_t, dskip_exp, batch, seq, d_inner, heads, reverse):
    t = xc.shape[0]
    nc = seq // CHUNK
    gn = (xc.shape[1] - d_inner) // 2
    bblk = d_inner // gn
    n = gn // SSM_GROUPS

    def rowblk(b, c):
        return b * nc + (nc - 1 - c if reverse else c)

    return pl.pallas_call(
        functools.partial(_ssd_kernel, reverse=reverse, heads=heads, add_skip=not reverse),
        grid=(batch, nc),
        in_specs=[
            pl.BlockSpec((CHUNK, d_inner), lambda b, c: (rowblk(b, c), 0)),
            pl.BlockSpec((CHUNK, gn), lambda b, c: (rowblk(b, c), bblk)),
            pl.BlockSpec((CHUNK, gn), lambda b, c: (rowblk(b, c), bblk + 1)),
            pl.BlockSpec((CHUNK, LANES), lambda b, c: (rowblk(b, c), 0)),
            pl.BlockSpec((CHUNK, LANES), lambda b, c: (rowblk(b, c), 0)),
            pl.BlockSpec((CHUNK, LANES), lambda b, c: (rowblk(b, c), 0)),
            pl.BlockSpec((1, d_inner), lambda b, c: (0, 0)),
        ],
        out_specs=pl.BlockSpec((CHUNK, d_inner), lambda b, c: (rowblk(b, c), 0)),
        out_shape=jax.ShapeDtypeStruct((t, d_inner), BF16),
        scratch_shapes=[pltpu.VMEM((heads // 2, n, 2 * d_inner // heads), F32)],
        compiler_params=_cparams(("parallel", "arbitrary")),
        name="ssd_bwd" if reverse else "ssd_fwd",
    )(xc, xc, xc, acum, src_t, wrow_t, dskip_exp)


def _mix_kernel(yf_ref, yb_ref, z_ref, u_ref, up_ref, un_ref, ga_ref, gb_ref, x_ref,
                snw_ref, wso_ref, wp_ref, ps_ref, wo_ref, fnw_ref, wr_ref,
                x1_ref, h2_ref, pt_ref, *, tiles_per_seq, seq, n_experts):
    tm = x_ref.shape[0]
    d_inner = z_ref.shape[1]
    gw = d_inner // SSM_GROUPS
    halo = up_ref.shape[0]
    sub = min(tm, 2 * CHUNK)
    pos0 = (pl.program_id(0) % tiles_per_seq) * tm
    pw = u_ref.shape[1] // len(POOL_WINDOWS)
    keep_prev = jnp.where(pos0 > 0, 1.0, 0.0)
    keep_next = jnp.where(pos0 + tm < seq, 1.0, 0.0)
    wb, wl = _split_bf16(wr_ref[...], 2)

    tq = _iota((sub, sub), 0)
    sq = _iota((sub, sub), 1)
    tp = _iota((sub, halo), 0)
    jp = _iota((sub, halo), 1)
    bands = []
    for win in POOL_WINDOWS:
        lo = win // 2
        hi = win - lo
        in_c = (sq >= tq - lo) & (sq < tq + hi)
        in_p = jp - halo >= tp - lo
        in_n = jp + sub < tp + hi
        bands.append((jnp.where(in_c, 1.0, 0.0).astype(BF16), in_p, in_n))

    for si in range(tm // sub):
        rows = slice(si * sub, (si + 1) * sub)
        at_top = si == 0
        at_bottom = si == tm // sub - 1

        z = z_ref[rows, :].astype(F32)
        y = (yf_ref[rows, :].astype(F32) + yb_ref[rows, :].astype(F32)) * (z * _sigmoid(z))
        parts = []
        for g in range(SSM_GROUPS):
            blk = y[:, g * gw:(g + 1) * gw]
            sc = lax.rsqrt(jnp.mean(blk * blk, axis=-1, keepdims=True) + EPS)
            parts.append((blk * sc * snw_ref[:, g * gw:(g + 1) * gw]).astype(BF16))
        y_a = _dot(jnp.concatenate(parts, axis=1), wso_ref[...])

        tpos = pos0 + si * sub + _iota((sub, 1), 0)
        pooled = []
        for gi, win in enumerate(POOL_WINDOWS):
            lo = win // 2
            hi = win - lo
            band_c, in_p, in_n = bands[gi]
            cg = slice(gi * pw, (gi + 1) * pw)
            ug = u_ref[rows, cg]
            u_prev = up_ref[:, cg] if at_top else u_ref[si * sub - halo:si * sub, cg]
            u_next = un_ref[:, cg] if at_bottom else u_ref[(si + 1) * sub:(si + 1) * sub + halo, cg]
            band_p = jnp.where(in_p, keep_prev if at_top else 1.0, 0.0).astype(BF16)
            band_n = jnp.where(in_n, keep_next if at_bottom else 1.0, 0.0).astype(BF16)
            s = _dot(band_c, ug) + _dot(band_p, u_prev) + _dot(band_n, u_next)
            cnt = (jnp.minimum(tpos + hi, seq) - jnp.maximum(tpos - lo, 0)).astype(F32)
            pg = (s / cnt - ug.astype(F32)).astype(BF16)
            pooled.append(_dot(pg, wp_ref[gi]))
        y_b = jnp.concatenate(pooled, axis=1) * ps_ref[...]

        merged = (_sigmoid(ga_ref[rows, :].astype(F32)) * y_a
                  + _sigmoid(gb_ref[rows, :].astype(F32)) * y_b)
        x1 = x_ref[rows, :] + _dot(merged.astype(BF16), wo_ref[...])
        x1_ref[rows, :] = x1

        h2 = x1 * lax.rsqrt(jnp.mean(x1 * x1, axis=-1, keepdims=True) + EPS) * fnw_ref[...]
        h2_ref[rows, :] = h2
        hb, hl = _split_bf16(h2, 2)
        logits = _dot(hb, wb) + _dot(hl, wb) + _dot(hb, wl)
        logits = jnp.where(_iota(logits.shape, 1) < n_experts, logits, -1e30)
        e = jnp.exp(logits - jnp.max(logits, axis=-1, keepdims=True))
        probs = e / jnp.sum(e, axis=-1, keepdims=True)
        pt_ref[:, rows] = probs.T[:n_experts, :]


def _mix(yf, yb, main, x2d, ssm_norm_w, w_ssm_out, w_pool, pool_scale, w_o, norm_ffn_w, w_router_pad,
         seq, d_inner, conv_ch, n_experts, tm):
    t, d = x2d.shape
    halo = BF16_ROWS
    rh = tm // halo
    nhalo = t // halo
    zb = 0
    ub = (d_inner + conv_ch) // d
    full = lambda shape: pl.BlockSpec(shape, lambda i: (0,) * len(shape))
    return pl.pallas_call(
        functools.partial(_mix_kernel, tiles_per_seq=seq // tm, seq=seq, n_experts=n_experts),
        grid=(t // tm,),
        in_specs=[
            pl.BlockSpec((tm, d_inner), lambda i: (i, 0)),
            pl.BlockSpec((tm, d_inner), lambda i: (i, 0)),
            pl.BlockSpec((tm, d_inner), lambda i: (i, zb)),
            pl.BlockSpec((tm, d), lambda i: (i, ub)),
            pl.BlockSpec((halo, d), lambda i: (jnp.maximum(i * rh - 1, 0), ub)),
            pl.BlockSpec((halo, d), lambda i: (jnp.minimum((i + 1) * rh, nhalo - 1), ub)),
            pl.BlockSpec((tm, d), lambda i: (i, ub + 1)),
            pl.BlockSpec((tm, d), lambda i: (i, ub + 2)),
            pl.BlockSpec((tm, d), lambda i: (i, 0)),
            full((1, d_inner)),
            full((d_inner, d)),
            full(w_pool.shape),
            full((1, d)),
            full((d, d)),
            full((1, d)),
            full((d, LANES)),
        ],
        out_specs=[
            pl.BlockSpec((tm, d), lambda i: (i, 0)),
            pl.BlockSpec((tm, d), lambda i: (i, 0)),
            pl.BlockSpec((n_experts, tm), lambda i: (0, i)),
        ],
        out_shape=[
            jax.ShapeDtypeStruct((t, d), F32),
            jax.ShapeDtypeStruct((t, d), F32),
            jax.ShapeDtypeStruct((n_experts, t), F32),
        ],
        compiler_params=_cparams(("parallel",)),
        name="mix",
    )(yf, yb, main, main, main, main, main, main, x2d,
      ssm_norm_w, w_ssm_out, w_pool, pool_scale, w_o, norm_ffn_w, w_router_pad)


def _route_kernel(p_ref, idx_ref, aff_ref, code_ref, off_ref, tot_ref, *, cap, ct):
    p = p_ref[0]
    r = p.shape[0]
    key = pltpu.bitcast(p, I32)

    def search(i, prefix):
        cand = prefix | lax.shift_left(jnp.int32(1), 30 - i)
        cnt = jnp.sum(jnp.where(key >= cand, 1, 0))
        return jnp.where(cnt >= cap, cand, prefix)

    tau = lax.fori_loop(0, 31, search, jnp.int32(0))

    lane_r = _iota((LANES, LANES), 0)
    lane_c = _iota((LANES, LANES), 1)
    tri_incl = jnp.where(lane_r <= lane_c, 1.0, 0.0).astype(BF16)
    rr = _iota((r, r), 0)
    rc = _iota((r, r), 1)
    tril_strict = jnp.where(rc < rr, 1.0, 0.0).astype(BF16)

    def cumsum(mask):
        mb = jnp.where(mask, 1.0, 0.0).astype(BF16)
        incl = _dot(mb, tri_incl)
        tot = jnp.broadcast_to(incl[:, LANES - 1:LANES], (r, LANES))
        off = _dot(tril_strict, tot.astype(BF16))
        return mb, incl, off, tot

    gt = key > tau
    eq = key == tau
    need = (cap - jnp.sum(jnp.where(gt, 1, 0))).astype(F32)
    mb_eq, incl_eq, off_eq, _ = cumsum(eq)
    rank_eq = off_eq + incl_eq - mb_eq.astype(F32)
    sel = gt | (eq & (rank_eq < need))
    mb, incl, off, tot = cumsum(sel)

    local = incl - mb.astype(F32)
    code_ref[0] = jnp.where(sel, local, -1.0).astype(I32)

    ones8 = jnp.ones((SUBLANES, LANES), BF16)
    tot_t = _dot_nt(ones8, mb)
    triu_strict = jnp.where(rr < rc, 1.0, 0.0).astype(BF16)
    off_t = _dot(tot_t.astype(BF16), triu_strict)
    off_ref[0] = off_t[0:1, :].astype(I32)
    tot_ref[0] = tot_t[0:1, :].astype(I32)

    incl_t = _dot_nt(jnp.where(lane_c <= lane_r, 1.0, 0.0).astype(BF16), mb).astype(BF16)
    p_t = _split_bf16(p.T, 3)
    incl_row = (off + tot)[:, 0:1]
    off_col = off[:, 0:1]
    for c0 in range(0, cap, ct):
        cvec = (c0 + _iota((1, ct), 1)).astype(F32)
        before = incl_row <= cvec
        rsel = jnp.sum(jnp.where(before, 1, 0), axis=0, keepdims=True)
        onehot = _iota((r, ct), 0) == rsel
        onehot_bf = jnp.where(onehot, 1.0, 0.0).astype(BF16)
        g_t = _dot(incl_t, onehot_bf)
        offc = jnp.sum(jnp.where(onehot, off_col, 0.0), axis=0, keepdims=True)
        target = cvec - offc
        lane = jnp.sum(jnp.where(g_t <= target, 1, 0), axis=0, keepdims=True)
        idx_ref[0, :, c0:c0 + ct] = rsel * LANES + lane
        a_t = _dot(p_t[0], onehot_bf) + _dot(p_t[1], onehot_bf) + _dot(p_t[2], onehot_bf)
        hit = _iota((LANES, ct), 0) == lane
        aff_ref[0, :, c0:c0 + ct] = jnp.sum(jnp.where(hit, a_t, 0.0), axis=0, keepdims=True)


def _route(probs_t, cap):
    e, t = probs_t.shape
    r = t // LANES
    ct = min(cap, 512)
    p3 = probs_t.reshape(e, r, LANES)
    return pl.pallas_call(
        functools.partial(_route_kernel, cap=cap, ct=ct),
        grid=(e,),
        in_specs=[pl.BlockSpec((1, r, LANES), lambda i: (i, 0, 0))],
        out_specs=[
            pl.BlockSpec((1, 1, cap), lambda i: (i, 0, 0)),
            pl.BlockSpec((1, 1, cap), lambda i: (i, 0, 0)),
            pl.BlockSpec((1, r, LANES), lambda i: (i, 0, 0)),
            pl.BlockSpec((1, 1, r), lambda i: (i, 0, 0)),
            pl.BlockSpec((1, 1, r), lambda i: (i, 0, 0)),
        ],
        out_shape=[
            jax.ShapeDtypeStruct((e, 1, cap), I32),
            jax.ShapeDtypeStruct((e, 1, cap), F32),
            jax.ShapeDtypeStruct((e, r, LANES), I32),
            jax.ShapeDtypeStruct((e, 1, r), I32),
            jax.ShapeDtypeStruct((e, 1, r), I32),
        ],
        compiler_params=_cparams(("parallel",)),
        name="route",
    )(p3)


SC_CORES = 2
SC_SUBCORES = 16
SC_GATHER_ROWS = 32


def _gather(h2, idx_flat):
    t, d = h2.shape
    nrows = idx_flat.shape[0]
    nc, ns = SC_CORES, SC_SUBCORES
    nw = nc * ns
    bw = SC_GATHER_ROWS
    per_w = nrows // nw
    nb = per_w // bw
    assert nrows % (nw * bw * 2) == 0
    mesh = plsc.VectorSubcoreMesh(core_axis_name="c", subcore_axis_name="s", num_cores=nc, num_subcores=ns)

    @functools.partial(
        pl.kernel, mesh=mesh, out_type=jax.ShapeDtypeStruct((nrows, d), h2.dtype),
        scratch_types=[pltpu.VMEM((nb, bw), I32), pltpu.VMEM((bw, d), h2.dtype), pltpu.VMEM((bw, d), h2.dtype),
                       pltpu.SemaphoreType.DMA, pltpu.SemaphoreType.DMA],
        name="gather")
    def run(table_hbm, idx_hbm, out_hbm, idx_v, rows_a, rows_b, sem_a, sem_b):
        wid = lax.axis_index("s") * nc + lax.axis_index("c")
        base = wid * per_w
        pltpu.sync_copy(idx_hbm.at[wid], idx_v)

        def gather(j, buf, sem):
            return pltpu.make_async_copy(table_hbm.at[idx_v.at[j]], buf, sem)

        gather(0, rows_a, sem_a).start()

        @pl.loop(0, nb, step=2)
        def _(j):
            gather(j + 1, rows_b, sem_b).start()
            gather(j, rows_a, sem_a).wait()
            pltpu.sync_copy(rows_a, out_hbm.at[pl.ds(base + j * bw, bw)])

            @pl.when(j + 2 < nb)
            def _():
                gather(j + 2, rows_a, sem_a).start()

            gather(j + 1, rows_b, sem_b).wait()
            pltpu.sync_copy(rows_b, out_hbm.at[pl.ds(base + (j + 1) * bw, bw)])

    return run(h2, idx_flat.reshape(nw, nb, bw))


def _ffn_kernel(xe_ref, aff_ref, wg_ref, wu_ref, wd_ref, out_ref, xb_ref, hid_ref, *, fc):
    xb_ref[...] = xe_ref[...].astype(BF16)
    for c in range(wg_ref.shape[1] // fc):
        g = _dot(xb_ref[...], wg_ref[:, c * fc:(c + 1) * fc])
        u = _dot(xb_ref[...], wu_ref[:, c * fc:(c + 1) * fc])
        hid_ref[:, c * fc:(c + 1) * fc] = (g * _sigmoid(g) * u).astype(BF16)
    y = _dot(hid_ref[...], wd_ref[...])
    for j in range(aff_ref.shape[0]):
        a = jnp.broadcast_to(aff_ref[j:j + 1, :], (LANES, LANES)).T[:, 0:1]
        out_ref[j * LANES:(j + 1) * LANES, :] = (y[j * LANES:(j + 1) * LANES, :] * a).astype(out_ref.dtype)


def _ffn(xe, aff2d, w_gate, w_up, w_down, cap, rs, fc):
    nrows, d = xe.shape
    e, _, ff = w_gate.shape
    rpe = cap // rs
    return pl.pallas_call(
        functools.partial(_ffn_kernel, fc=fc),
        grid=(e, rpe),
        in_specs=[
            pl.BlockSpec((rs, d), lambda ei, r: (ei * rpe + r, 0)),
            pl.BlockSpec((None, rs // LANES, LANES), lambda ei, r: (ei * rpe + r, 0, 0)),
            pl.BlockSpec((None, d, ff), lambda ei, r: (ei, 0, 0)),
            pl.BlockSpec((None, d, ff), lambda ei, r: (ei, 0, 0)),
            pl.BlockSpec((None, ff, d), lambda ei, r: (ei, 0, 0)),
        ],
        out_specs=pl.BlockSpec((rs, d), lambda ei, r: (ei * rpe + r, 0)),
        out_shape=jax.ShapeDtypeStruct((nrows, d), BF16),
        scratch_shapes=[pltpu.VMEM((rs, d), BF16), pltpu.VMEM((rs, ff), BF16)],
        compiler_params=_cparams(("parallel", "arbitrary")),
        name="expert_ffn",
    )(xe, aff2d.reshape(nrows // rs, rs // LANES, LANES), w_gate, w_up, w_down)


def _combine_kernel(off_ref, tot_ref, code_ref, x1_ref, fw_ref, ye_hbm, out_ref, buf_ref, sem,
                    *, cap, n_experts, qc, final_norm):
    blk = pl.program_id(0)
    nblk = pl.num_programs(0)
    tb = x1_ref.shape[0]

    ch = BF16_ROWS

    def layout(b):
        qbase = jnp.int32(0)
        chunks, shifts = [], []
        for e in range(n_experts):
            s = off_ref[e, b]
            n = tot_ref[e, b]
            c_lo = s // ch
            nch = jnp.where(n > 0, (s + n - 1) // ch - c_lo + 1, 0)
            chunks.append((c_lo, nch, qbase))
            shifts.append(qbase + (s - c_lo * ch))
            qbase = qbase + nch * ch
        return chunks, shifts, qbase

    def chunk_copy(src, dst, slot):
        return pltpu.make_async_copy(ye_hbm.at[pl.ds(src, ch), :],
                                     buf_ref.at[slot, pl.ds(dst, ch), :], sem.at[slot])

    def fetch(b, slot):
        chunks, _, _ = layout(b)
        for e, (c_lo, nch, qbase) in enumerate(chunks):
            def issue(i, carry, e=e, c_lo=c_lo, qbase=qbase):
                src = pl.multiple_of(e * cap + (c_lo + i) * ch, ch)
                dst = pl.multiple_of(qbase + i * ch, ch)
                chunk_copy(src, dst, slot).start()
                return carry

            lax.fori_loop(0, nch, issue, 0)

    slot = blk % 2

    @pl.when(blk == 0)
    def _():
        buf_ref[...] = jnp.zeros_like(buf_ref)
        fetch(blk, slot)

    @pl.when(blk + 1 < nblk)
    def _():
        fetch(blk + 1, 1 - slot)

    _, shifts, qtot = layout(blk)

    def drain(i, carry):
        chunk_copy(0, 0, slot).wait()
        return carry

    lax.fori_loop(0, qtot // ch, drain, 0)

    code_t = code_ref[:, 0, :].astype(F32).T
    nq = (qtot + qc - 1) // qc
    tgt = []
    for e in range(n_experts):
        ce = code_t[:, e:e + 1].astype(I32)
        tgt.append(jnp.broadcast_to(jnp.where(ce >= 0, ce + shifts[e], -1), (tb, LANES)))
    lane = _iota((tb, LANES), 1)

    def seg_sum(k, acc):
        q0 = pl.multiple_of(k * qc, qc)
        cols = []
        for c in range(qc // LANES):
            qi = q0 + c * LANES + lane
            s_col = jnp.zeros((tb, LANES), F32)
            for e in range(n_experts):
                s_col = jnp.where(tgt[e] == qi, 1.0, s_col)
            cols.append(s_col.astype(BF16))
        return acc + _dot(jnp.concatenate(cols, axis=1), buf_ref[slot, pl.ds(q0, qc), :])

    ffn = lax.fori_loop(0, nq, seg_sum, jnp.zeros(x1_ref.shape, F32))
    x2 = x1_ref[...] + ffn
    if final_norm:
        x2 = x2 * lax.rsqrt(jnp.mean(x2 * x2, axis=-1, keepdims=True) + EPS) * fw_ref[...]
    out_ref[...] = x2


def _combine(off, tot, code, x1, ye, final_w, cap, final_norm):
    t, d = x1.shape
    e = code.shape[0]
    tb = LANES
    qc = 256
    max_rows = e * (tb + 2 * BF16_ROWS)
    buf_rows = ((max_rows + qc - 1) // qc) * qc
    return pl.pallas_call(
        functools.partial(_combine_kernel, cap=cap, n_experts=e, qc=qc, final_norm=final_norm),
        grid_spec=pltpu.PrefetchScalarGridSpec(
            num_scalar_prefetch=2,
            grid=(t // tb,),
            in_specs=[
                pl.BlockSpec((e, None, 1, tb), lambda i, *_: (0, i, 0, 0)),
                pl.BlockSpec((tb, d), lambda i, *_: (i, 0)),
                pl.BlockSpec((1, d), lambda i, *_: (0, 0)),
                pl.BlockSpec(memory_space=pl.ANY),
            ],
            out_specs=pl.BlockSpec((tb, d), lambda i, *_: (i, 0)),
            scratch_shapes=[pltpu.VMEM((2, buf_rows, d), ye.dtype), pltpu.SemaphoreType.DMA((2,))],
        ),
        out_shape=jax.ShapeDtypeStruct((t, d), F32),
        compiler_params=_cparams(("arbitrary",)),
        name="combine",
    )(off, tot, code.reshape(e, t // tb, 1, tb), x1, final_w, ye)


def _pick(n, prefs):
    for p in prefs:
        if n % p == 0:
            return p
    raise ValueError(f"no tile size for {n}")


def _layer_front(x, lw):
    batch, seq, d = x.shape
    t = batch * seq
    heads = lw["dt_bias_f"].shape[-1]
    d_inner = lw["ssm_norm_w"].shape[-1]
    conv_ch = lw["conv_w"].shape[-1]
    n_experts = lw["w_router"].shape[-1]
    cap = CAPACITY_FACTOR * t // n_experts
    assert seq % CHUNK == 0 and cap % LANES == 0 and 2 * heads <= LANES and CHUNK == LANES

    x2d = x.reshape(t, d)
    main, acum, src_t, wrow_t = _in_proj(
        x2d, lw["norm_mix_w"], lw["w_main"], lw["w_dt"], lw["dt_bias_all"], lw["a_log_all"], heads,
        _pick(t, (512, 256, 128)), _pick(lw["w_main"].shape[1], (1024, 512)))
    xc = _conv(main, lw["conv_w"], lw["conv_b"], seq, d_inner,
               _pick(seq, (512, 256, 128)), _pick(conv_ch, (1024, 512)))
    yf = _ssd(xc, acum, src_t, wrow_t, lw["d_skip_exp"], batch, seq, d_inner, heads, reverse=False)
    yb = _ssd(xc, acum, src_t, wrow_t, lw["d_skip_exp"], batch, seq, d_inner, heads, reverse=True)
    x1, h2, probs_t = _mix(yf, yb, main, x2d, lw["ssm_norm_w"], lw["w_ssm_out"], lw["w_pool"],
                           lw["pool_scale"], lw["w_o"], lw["norm_ffn_w"], lw["w_router_pad"],
                           seq, d_inner, conv_ch, n_experts, _pick(seq, (512, 256, 128)))
    idx, aff, code, off, tot = _route(probs_t, cap)
    xe = _gather(h2, idx.reshape(n_experts * cap))
    return dict(shape=x.shape, cap=cap, x1=x1, xe=xe, aff=aff, code=code, off=off, tot=tot)


def _layer_back(fr, lw, final_w, final_norm):
    batch, seq, d = fr["shape"]
    t = batch * seq
    cap = fr["cap"]
    n_experts = fr["code"].shape[0]
    ye = _ffn(fr["xe"], fr["aff"].reshape(n_experts * cap // LANES, LANES), lw["w_gate"], lw["w_up"],
              lw["w_down"], cap, _pick(cap, (512, 256, 128)), _pick(lw["w_gate"].shape[-1], (256, 128)))
    r = t // LANES
    out = _combine(fr["off"].reshape(n_experts, r), fr["tot"].reshape(n_experts, r), fr["code"], fr["x1"],
                   ye, final_w, cap, final_norm)
    return out.reshape(batch, seq, d)


def _prep_layer(i, norm_mix_w, w_in, conv_w, conv_b, dt_bias_f, dt_bias_b, a_log_f, a_log_b, d_skip,
                ssm_norm_w, w_ssm_out, w_pool, pool_scale, w_o, norm_ffn_w, w_router, w_gate, w_up, w_down):
    d = w_in.shape[1]
    heads = dt_bias_f.shape[-1]
    d_inner = ssm_norm_w.shape[-1]
    conv_ch = conv_w.shape[-1]
    p = d_inner // heads
    o2 = d_inner + conv_ch
    o4 = o2 + 2 * heads
    wi = w_in[i]
    w_main = jnp.concatenate([wi[:, :o2], wi[:, o4:]], axis=1).astype(BF16)
    w_dt = jnp.pad(wi[:, o2:o4], ((0, 0), (0, LANES - 2 * heads)))
    row = lambda v: v.reshape(1, -1).astype(F32)
    padl = lambda v: jnp.pad(v.reshape(1, -1).astype(F32), ((0, 0), (0, LANES - v.size)))
    return {
        "norm_mix_w": row(norm_mix_w[i]), "w_main": w_main, "w_dt": w_dt,
        "conv_w": conv_w[i], "conv_b": row(conv_b[i]),
        "dt_bias_f": dt_bias_f[i],
        "dt_bias_all": padl(jnp.concatenate([dt_bias_f[i], dt_bias_b[i]])),
        "a_log_all": padl(jnp.concatenate([a_log_f[i], a_log_b[i]])),
        "d_skip_exp": row(jnp.repeat(d_skip[i], p)),
        "ssm_norm_w": row(ssm_norm_w[i]), "w_ssm_out": w_ssm_out[i].astype(BF16),
        "w_pool": w_pool[i].astype(BF16), "pool_scale": row(pool_scale[i]),
        "w_o": w_o[i].astype(BF16), "norm_ffn_w": row(norm_ffn_w[i]),
        "w_router": w_router[i],
        "w_router_pad": jnp.pad(w_router[i].astype(F32), ((0, 0), (0, LANES - w_router.shape[-1]))),
        "w_gate": w_gate[i].astype(BF16), "w_up": w_up[i].astype(BF16), "w_down": w_down[i].astype(BF16),
    }


def kernel(x_prompt, x_sample, norm_mix_w, w_in, conv_w, conv_b, dt_bias_f, dt_bias_b, a_log_f, a_log_b,
           d_skip, ssm_norm_w, w_ssm_out, w_pool, pool_scale, w_o, norm_ffn_w, w_router, w_gate, w_up,
           w_down, norm_final_w):
    depth = w_in.shape[0]
    layers = [_prep_layer(i, norm_mix_w, w_in, conv_w, conv_b, dt_bias_f, dt_bias_b, a_log_f, a_log_b,
                          d_skip, ssm_norm_w, w_ssm_out, w_pool, pool_scale, w_o, norm_ffn_w, w_router,
                          w_gate, w_up, w_down) for i in range(depth)]
    final_w = norm_final_w.reshape(1, -1).astype(F32)

    xs = [x_prompt, x_sample]
    for i, lw in enumerate(layers):
        fronts = [_layer_front(x, lw) for x in xs]
        xs = [_layer_back(fr, lw, final_w, final_norm=(i == depth - 1)) for fr in fronts]
    return tuple(xs)
```

```python
import functools

import jax
import jax.numpy as jnp
from jax import lax
from jax.experimental import pallas as pl
from jax.experimental.pallas import tpu as pltpu
from jax.experimental.pallas import tpu_sc as plsc

F32 = jnp.float32
BF16 = jnp.bfloat16
I32 = jnp.int32

EPS = 1e-6
LOG2E = 1.4426950408889634
CHUNK = 128
LANES = 128
SUBLANES = 8
BF16_ROWS = 16
SSM_GROUPS = 8
POOL_WINDOWS = (2, 4, 8, 16)
CAPACITY_FACTOR = 2
VMEM_LIMIT = 56 * 1024 * 1024


def _cparams(sem):
    return pltpu.CompilerParams(dimension_semantics=sem, vmem_limit_bytes=VMEM_LIMIT)


def _iota(shape, dim):
    return lax.broadcasted_iota(I32, shape, dim)


def _sigmoid(x):
    return 1.0 / (1.0 + jnp.exp(-x))


def _split_bf16(x, terms):
    parts = []
    rem = x
    for _ in range(terms):
        p = rem.astype(BF16)
        parts.append(p)
        rem = rem - p.astype(F32)
    return parts


def _dot(a, b):
    return jnp.dot(a, b, preferred_element_type=F32)


def _dot_nt(a, b):
    return lax.dot_general(a, b, (((1,), (1,)), ((), ())), preferred_element_type=F32)


def _dot_exact_lhs(mask_bf, x, terms=3):
    out = None
    for p in _split_bf16(x, terms):
        d = _dot(mask_bf, p)
        out = d if out is None else out + d
    return out


def _in_proj_kernel(x_ref, nw_ref, w_ref, wdt_ref, bias_ref, alog_ref, main_ref, acum_ref, src_ref, wrow_ref,
                    h_ref, *, tn, heads):
    x = x_ref[...]
    h = x * lax.rsqrt(jnp.mean(x * x, axis=-1, keepdims=True) + EPS) * nw_ref[...]
    hb, hl = _split_bf16(h, 2)
    h_ref[...] = hb
    wb, wl = _split_bf16(wdt_ref[...], 2)
    v = _dot(hb, wb) + _dot(hl, wb) + _dot(hb, wl) + bias_ref[...]
    dt = jnp.maximum(v, 0.0) + jnp.log1p(jnp.exp(-jnp.abs(v)))

    L = CHUNK
    neg_a = -jnp.exp(alog_ref[...])
    row = _iota((L, L), 0)
    col = _iota((L, L), 1)
    tril = jnp.where(col <= row, 1.0, 0.0).astype(BF16)
    triu = jnp.where(col >= row, 1.0, 0.0).astype(BF16)
    fwd_col = col < heads
    fwd_row = _iota((LANES, 1), 0) < heads
    for k in range(x_ref.shape[0] // L):
        rows = slice(k * L, (k + 1) * L)
        dtk = dt[rows, :]
        da = dtk * neg_a
        acum = jnp.where(fwd_col, _dot_exact_lhs(tril, da), _dot_exact_lhs(triu, da))
        acum_t = acum.T
        a_last_t = jnp.where(fwd_row, acum_t[:, L - 1:L], acum_t[:, 0:1])
        src_t = acum_t - jnp.log(dtk).T
        acum_ref[rows, :] = acum * LOG2E
        src_ref[rows, :] = src_t * LOG2E
        wrow_ref[rows, :] = jnp.exp(a_last_t - src_t)

    for j in range(w_ref.shape[1] // tn):
        main_ref[:, j * tn:(j + 1) * tn] = _dot(h_ref[...], w_ref[:, j * tn:(j + 1) * tn]).astype(main_ref.dtype)


def _in_proj(x2d, norm_w, w_main, w_dt, dt_bias, alog_all, heads, tm, tn):
    t, d = x2d.shape
    ncols = w_main.shape[1]
    const = lambda shape: pl.BlockSpec(shape, lambda i: (0, 0), pipeline_mode=pl.Buffered(1))
    table = pl.BlockSpec((tm, LANES), lambda i: (i, 0))
    return pl.pallas_call(
        functools.partial(_in_proj_kernel, tn=tn, heads=heads),
        grid=(t // tm,),
        in_specs=[
            pl.BlockSpec((tm, d), lambda i: (i, 0)),
            const((1, d)),
            const((d, ncols)),
            const((d, LANES)),
            const((1, LANES)),
            const((1, LANES)),
        ],
        out_specs=[pl.BlockSpec((tm, ncols), lambda i: (i, 0)), table, table, table],
        out_shape=[jax.ShapeDtypeStruct((t, ncols), BF16)] + [jax.ShapeDtypeStruct((t, LANES), F32)] * 3,
        scratch_shapes=[pltpu.VMEM((tm, d), BF16)],
        compiler_params=_cparams(("parallel",)),
        name="in_proj",
    )(x2d, norm_w, w_main, w_dt, dt_bias, alog_all)


def _conv_kernel(cur_ref, prev_ref, next_ref, w_ref, b_ref, out_ref, buf_ref, *, tiles_per_seq):
    i = pl.program_id(0)
    tr = cur_ref.shape[0]
    halo = prev_ref.shape[0]
    pos = i % tiles_per_seq
    keep_prev = jnp.where(pos == 0, 0.0, 1.0)
    keep_next = jnp.where(pos == tiles_per_seq - 1, 0.0, 1.0)
    buf_ref[0:halo, :] = (prev_ref[...].astype(F32) * keep_prev).astype(BF16)
    buf_ref[halo:halo + tr, :] = cur_ref[...]
    buf_ref[halo + tr:halo + tr + halo, :] = (next_ref[...].astype(F32) * keep_next).astype(BF16)
    k = w_ref.shape[0]
    pad = k // 2
    win = CHUNK + 2 * halo
    ti = _iota((CHUNK, win), 0)
    ji = _iota((CHUNK, win), 1)
    shift = {j: jnp.where(ji == ti + halo + (j - pad), 1.0, 0.0).astype(BF16) for j in range(k) if j != pad}
    for b in range(tr // CHUNK):
        xw = buf_ref[b * CHUNK:b * CHUNK + win, :]
        acc = b_ref[...] + w_ref[pad:pad + 1, :] * cur_ref[b * CHUNK:(b + 1) * CHUNK, :].astype(F32)
        for j in shift:
            acc = acc + w_ref[j:j + 1, :] * _dot(shift[j], xw)
        out_ref[b * CHUNK:(b + 1) * CHUNK, :] = (acc * _sigmoid(acc)).astype(out_ref.dtype)


def _conv(main, conv_w, conv_b, seq, col0, tr, tc):
    t = main.shape[0]
    k, ch = conv_w.shape
    halo = BF16_ROWS
    cb0 = col0 // tc
    nhalo = t // halo
    rh = tr // halo
    return pl.pallas_call(
        functools.partial(_conv_kernel, tiles_per_seq=seq // tr),
        grid=(t // tr, ch // tc),
        in_specs=[
            pl.BlockSpec((tr, tc), lambda i, j: (i, cb0 + j)),
            pl.BlockSpec((halo, tc), lambda i, j: (jnp.maximum(i * rh - 1, 0), cb0 + j)),
            pl.BlockSpec((halo, tc), lambda i, j: (jnp.minimum((i + 1) * rh, nhalo - 1), cb0 + j)),
            pl.BlockSpec((k, tc), lambda i, j: (0, j)),
            pl.BlockSpec((1, tc), lambda i, j: (0, j)),
        ],
        out_specs=pl.BlockSpec((tr, tc), lambda i, j: (i, j)),
        out_shape=jax.ShapeDtypeStruct((t, ch), BF16),
        scratch_shapes=[pltpu.VMEM((tr + 2 * halo, tc), BF16)],
        compiler_params=_cparams(("parallel", "parallel")),
        name="conv_silu",
    )(main, main, main, conv_w, conv_b)


def _ssd_kernel(xs_ref, b_ref, c_ref, acum_ref, src_ref, wrow_ref, dskip_ref, y_ref, state_ref,
                *, reverse, heads, add_skip):
    @pl.when(pl.program_id(1) == 0)
    def _():
        state_ref[...] = jnp.zeros_like(state_ref)

    L = CHUNK
    cpb = xs_ref.shape[0] // L
    d_inner = xs_ref.shape[1]
    p = d_inner // heads
    hpg = heads // SSM_GROUPS
    n = b_ref.shape[1] // SSM_GROUPS
    assert n == L
    assert hpg % 2 == 0
    off = heads if reverse else 0
    first = _iota((L, 2 * p), 1) < p

    row = _iota((L, L), 0)
    col = _iota((L, L), 1)
    mask = (col >= row) if reverse else (col <= row)
    last = 0 if reverse else L - 1

    def chunk(k, carry):
        r0 = pl.multiple_of(((cpb - 1 - k) if reverse else k) * L, L)
        rows = pl.ds(r0, L)
        acum = acum_ref[rows, :]
        src_t = src_ref[rows, :]
        wrow_t = wrow_ref[rows, :]
        dchunk = jnp.exp2(acum[last:last + 1, :])

        for g in range(SSM_GROUPS):
            bg = b_ref[rows, g * n:(g + 1) * n]
            cg = c_ref[rows, g * n:(g + 1) * n]
            cgf = cg.astype(F32)
            cb = _dot_nt(cg, bg)
            btf = bg.astype(F32).T
            for j in range(0, hpg, 2):
                hh0 = g * hpg + j
                cols = slice(hh0 * p, (hh0 + 2) * p)
                xpair = xs_ref[rows, cols]
                st = state_ref[hh0 // 2]
                rhs = jnp.concatenate([xpair, st.astype(BF16)], axis=0)
                ys, ds, dch = [], [], []
                for hh in (hh0, hh0 + 1):
                    h = off + hh
                    colb = jnp.broadcast_to(acum[:, h:h + 1], (L, L))
                    m = (cb * jnp.exp2(jnp.where(mask, colb - src_t[h:h + 1, :], -1e30))).astype(BF16)
                    cs = (cgf * jnp.exp2(colb)).astype(BF16)
                    ys.append(_dot(jnp.concatenate([m, cs], axis=1), rhs))
                    bts = (btf * wrow_t[h:h + 1, :]).astype(BF16)
                    ds.append(_dot(bts, xpair))
                    dch.append(dchunk[:, h:h + 1])
                y = jnp.where(first, ys[0], ys[1])
                state_ref[hh0 // 2] = (st * jnp.where(first[0:1, :], dch[0], dch[1])
                                       + jnp.where(first, ds[0], ds[1]))
                if add_skip:
                    y = y + xpair.astype(F32) * dskip_ref[:, cols]
                y_ref[rows, cols] = y.astype(y_ref.dtype)
        return carry

    lax.fori_loop(0, cpb, chunk, 0)


def _ssd(xc, acum, src_t, wrow_t, dskip_exp, batch, seq, d_inner, heads, reverse, cpb):
    t = xc.shape[0]
    nb = seq // (cpb * CHUNK)
    rb = cpb * CHUNK
    gn = (xc.shape[1] - d_inner) // 2
    bblk = d_inner // gn
    n = gn // SSM_GROUPS

    def rowblk(b, c):
        return b * nb + (nb - 1 - c if reverse else c)

    return pl.pallas_call(
        functools.partial(_ssd_kernel, reverse=reverse, heads=heads, add_skip=not reverse),
        grid=(batch, nb),
        in_specs=[
            pl.BlockSpec((rb, d_inner), lambda b, c: (rowblk(b, c), 0)),
            pl.BlockSpec((rb, gn), lambda b, c: (rowblk(b, c), bblk)),
            pl.BlockSpec((rb, gn), lambda b, c: (rowblk(b, c), bblk + 1)),
            pl.BlockSpec((rb, LANES), lambda b, c: (rowblk(b, c), 0)),
            pl.BlockSpec((rb, LANES), lambda b, c: (rowblk(b, c), 0)),
            pl.BlockSpec((rb, LANES), lambda b, c: (rowblk(b, c), 0)),
            pl.BlockSpec((1, d_inner), lambda b, c: (0, 0)),
        ],
        out_specs=pl.BlockSpec((rb, d_inner), lambda b, c: (rowblk(b, c), 0)),
        out_shape=jax.ShapeDtypeStruct((t, d_inner), BF16),
        scratch_shapes=[pltpu.VMEM((heads // 2, n, 2 * d_inner // heads), F32)],
        compiler_params=_cparams(("parallel", "arbitrary")),
        name="ssd_bwd" if reverse else "ssd_fwd",
    )(xc, xc, xc, acum, src_t, wrow_t, dskip_exp)


def _mix_kernel(yf_ref, yb_ref, z_ref, u_ref, up_ref, un_ref, ga_ref, gb_ref, x_ref,
                snw_ref, wso_ref, wp_ref, ps_ref, wo_ref, fnw_ref, wr_ref,
                x1_ref, h2_ref, pt_ref, *, tiles_per_seq, seq, n_experts):
    tm = x_ref.shape[0]
    d_inner = z_ref.shape[1]
    gw = d_inner // SSM_GROUPS
    halo = up_ref.shape[0]
    sub = min(tm, 2 * CHUNK)
    pos0 = (pl.program_id(0) % tiles_per_seq) * tm
    pw = u_ref.shape[1] // len(POOL_WINDOWS)
    keep_prev = jnp.where(pos0 > 0, 1.0, 0.0)
    keep_next = jnp.where(pos0 + tm < seq, 1.0, 0.0)
    wb, wl = _split_bf16(wr_ref[...], 2)

    tq = _iota((sub, sub), 0)
    sq = _iota((sub, sub), 1)
    tp = _iota((sub, halo), 0)
    jp = _iota((sub, halo), 1)
    bands = []
    for win in POOL_WINDOWS:
        lo = win // 2
        hi = win - lo
        in_c = (sq >= tq - lo) & (sq < tq + hi)
        in_p = jp - halo >= tp - lo
        in_n = jp + sub < tp + hi
        bands.append((jnp.where(in_c, 1.0, 0.0).astype(BF16), in_p, in_n))

    for si in range(tm // sub):
        rows = slice(si * sub, (si + 1) * sub)
        at_top = si == 0
        at_bottom = si == tm // sub - 1

        z = z_ref[rows, :].astype(F32)
        y = (yf_ref[rows, :].astype(F32) + yb_ref[rows, :].astype(F32)) * (z * _sigmoid(z))
        parts = []
        for g in range(SSM_GROUPS):
            blk = y[:, g * gw:(g + 1) * gw]
            sc = lax.rsqrt(jnp.mean(blk * blk, axis=-1, keepdims=True) + EPS)
            parts.append((blk * sc * snw_ref[:, g * gw:(g + 1) * gw]).astype(BF16))
        y_a = _dot(jnp.concatenate(parts, axis=1), wso_ref[...])

        tpos = pos0 + si * sub + _iota((sub, 1), 0)
        pooled = []
        for gi, win in enumerate(POOL_WINDOWS):
            lo = win // 2
            hi = win - lo
            band_c, in_p, in_n = bands[gi]
            cg = slice(gi * pw, (gi + 1) * pw)
            ug = u_ref[rows, cg]
            u_prev = up_ref[:, cg] if at_top else u_ref[si * sub - halo:si * sub, cg]
            u_next = un_ref[:, cg] if at_bottom else u_ref[(si + 1) * sub:(si + 1) * sub + halo, cg]
            band_p = jnp.where(in_p, keep_prev if at_top else 1.0, 0.0).astype(BF16)
            band_n = jnp.where(in_n, keep_next if at_bottom else 1.0, 0.0).astype(BF16)
            s = _dot(band_c, ug) + _dot(band_p, u_prev) + _dot(band_n, u_next)
            cnt = (jnp.minimum(tpos + hi, seq) - jnp.maximum(tpos - lo, 0)).astype(F32)
            pg = (s / cnt - ug.astype(F32)).astype(BF16)
            pooled.append(_dot(pg, wp_ref[gi]))
        y_b = jnp.concatenate(pooled, axis=1) * ps_ref[...]

        merged = (_sigmoid(ga_ref[rows, :].astype(F32)) * y_a
                  + _sigmoid(gb_ref[rows, :].astype(F32)) * y_b)
        x1 = x_ref[rows, :] + _dot(merged.astype(BF16), wo_ref[...])
        x1_ref[rows, :] = x1

        h2 = x1 * lax.rsqrt(jnp.mean(x1 * x1, axis=-1, keepdims=True) + EPS) * fnw_ref[...]
        h2_ref[rows, :] = h2
        hb, hl = _split_bf16(h2, 2)
        logits = _dot(hb, wb) + _dot(hl, wb) + _dot(hb, wl)
        logits = jnp.where(_iota(logits.shape, 1) < n_experts, logits, -1e30)
        e = jnp.exp(logits - jnp.max(logits, axis=-1, keepdims=True))
        probs = e / jnp.sum(e, axis=-1, keepdims=True)
        pt_ref[:, rows] = probs.T[:n_experts, :]


def _mix(yf, yb, main, x2d, ssm_norm_w, w_ssm_out, w_pool, pool_scale, w_o, norm_ffn_w, w_router_pad,
         seq, d_inner, conv_ch, n_experts, tm):
    t, d = x2d.shape
    halo = BF16_ROWS
    rh = tm // halo
    nhalo = t // halo
    zb = 0
    ub = (d_inner + conv_ch) // d
    full = lambda shape: pl.BlockSpec(shape, lambda i: (0,) * len(shape))
    return pl.pallas_call(
        functools.partial(_mix_kernel, tiles_per_seq=seq // tm, seq=seq, n_experts=n_experts),
        grid=(t // tm,),
        in_specs=[
            pl.BlockSpec((tm, d_inner), lambda i: (i, 0)),
            pl.BlockSpec((tm, d_inner), lambda i: (i, 0)),
            pl.BlockSpec((tm, d_inner), lambda i: (i, zb)),
            pl.BlockSpec((tm, d), lambda i: (i, ub)),
            pl.BlockSpec((halo, d), lambda i: (jnp.maximum(i * rh - 1, 0), ub)),
            pl.BlockSpec((halo, d), lambda i: (jnp.minimum((i + 1) * rh, nhalo - 1), ub)),
            pl.BlockSpec((tm, d), lambda i: (i, ub + 1)),
            pl.BlockSpec((tm, d), lambda i: (i, ub + 2)),
            pl.BlockSpec((tm, d), lambda i: (i, 0)),
            full((1, d_inner)),
            full((d_inner, d)),
            full(w_pool.shape),
            full((1, d)),
            full((d, d)),
            full((1, d)),
            full((d, LANES)),
        ],
        out_specs=[
            pl.BlockSpec((tm, d), lambda i: (i, 0)),
            pl.BlockSpec((tm, d), lambda i: (i, 0)),
            pl.BlockSpec((n_experts, tm), lambda i: (0, i)),
        ],
        out_shape=[
            jax.ShapeDtypeStruct((t, d), F32),
            jax.ShapeDtypeStruct((t, d), F32),
            jax.ShapeDtypeStruct((n_experts, t), F32),
        ],
        compiler_params=_cparams(("parallel",)),
        name="mix",
    )(yf, yb, main, main, main, main, main, main, x2d,
      ssm_norm_w, w_ssm_out, w_pool, pool_scale, w_o, norm_ffn_w, w_router_pad)


def _route_kernel(p_ref, idx_ref, aff_ref, code_ref, off_ref, tot_ref, *, cap, ct):
    p = p_ref[0]
    r = p.shape[0]
    key = pltpu.bitcast(p, I32)

    def search(i, prefix):
        cand = prefix | lax.shift_left(jnp.int32(1), 30 - i)
        cnt = jnp.sum(jnp.where(key >= cand, 1, 0))
        return jnp.where(cnt >= cap, cand, prefix)

    tau = lax.fori_loop(0, 31, search, jnp.int32(0))

    lane_r = _iota((LANES, LANES), 0)
    lane_c = _iota((LANES, LANES), 1)
    tri_incl = jnp.where(lane_r <= lane_c, 1.0, 0.0).astype(BF16)
    rr = _iota((r, r), 0)
    rc = _iota((r, r), 1)
    tril_strict = jnp.where(rc < rr, 1.0, 0.0).astype(BF16)

    def cumsum(mask):
        mb = jnp.where(mask, 1.0, 0.0).astype(BF16)
        incl = _dot(mb, tri_incl)
        tot = jnp.broadcast_to(incl[:, LANES - 1:LANES], (r, LANES))
        off = _dot(tril_strict, tot.astype(BF16))
        return mb, incl, off, tot

    gt = key > tau
    eq = key == tau
    need = (cap - jnp.sum(jnp.where(gt, 1, 0))).astype(F32)
    mb_eq, incl_eq, off_eq, _ = cumsum(eq)
    rank_eq = off_eq + incl_eq - mb_eq.astype(F32)
    sel = gt | (eq & (rank_eq < need))
    mb, incl, off, tot = cumsum(sel)

    local = incl - mb.astype(F32)
    code_ref[0] = jnp.where(sel, local, -1.0).astype(I32)

    ones8 = jnp.ones((SUBLANES, LANES), BF16)
    tot_t = _dot_nt(ones8, mb)
    triu_strict = jnp.where(rr < rc, 1.0, 0.0).astype(BF16)
    off_t = _dot(tot_t.astype(BF16), triu_strict)
    off_ref[0] = off_t[0:1, :].astype(I32)
    tot_ref[0] = tot_t[0:1, :].astype(I32)

    incl_t = _dot_nt(jnp.where(lane_c <= lane_r, 1.0, 0.0).astype(BF16), mb).astype(BF16)
    p_t = _split_bf16(p.T, 3)
    incl_row = (off + tot)[:, 0:1]
    off_col = off[:, 0:1]
    for c0 in range(0, cap, ct):
        cvec = (c0 + _iota((1, ct), 1)).astype(F32)
        before = incl_row <= cvec
        rsel = jnp.sum(jnp.where(before, 1, 0), axis=0, keepdims=True)
        onehot = _iota((r, ct), 0) == rsel
        onehot_bf = jnp.where(onehot, 1.0, 0.0).astype(BF16)
        g_t = _dot(incl_t, onehot_bf)
        offc = jnp.sum(jnp.where(onehot, off_col, 0.0), axis=0, keepdims=True)
        target = cvec - offc
        lane = jnp.sum(jnp.where(g_t <= target, 1, 0), axis=0, keepdims=True)
        idx_ref[0, :, c0:c0 + ct] = rsel * LANES + lane
        a_t = _dot(p_t[0], onehot_bf) + _dot(p_t[1], onehot_bf) + _dot(p_t[2], onehot_bf)
        hit = _iota((LANES, ct), 0) == lane
        aff_ref[0, :, c0:c0 + ct] = jnp.sum(jnp.where(hit, a_t, 0.0), axis=0, keepdims=True)


def _route(probs_t, cap):
    e, t = probs_t.shape
    r = t // LANES
    ct = min(cap, 512)
    p3 = probs_t.reshape(e, r, LANES)
    return pl.pallas_call(
        functools.partial(_route_kernel, cap=cap, ct=ct),
        grid=(e,),
        in_specs=[pl.BlockSpec((1, r, LANES), lambda i: (i, 0, 0))],
        out_specs=[
            pl.BlockSpec((1, 1, cap), lambda i: (i, 0, 0)),
            pl.BlockSpec((1, 1, cap), lambda i: (i, 0, 0)),
            pl.BlockSpec((1, r, LANES), lambda i: (i, 0, 0)),
            pl.BlockSpec((1, 1, r), lambda i: (i, 0, 0)),
            pl.BlockSpec((1, 1, r), lambda i: (i, 0, 0)),
        ],
        out_shape=[
            jax.ShapeDtypeStruct((e, 1, cap), I32),
            jax.ShapeDtypeStruct((e, 1, cap), F32),
            jax.ShapeDtypeStruct((e, r, LANES), I32),
            jax.ShapeDtypeStruct((e, 1, r), I32),
            jax.ShapeDtypeStruct((e, 1, r), I32),
        ],
        compiler_params=_cparams(("parallel",)),
        name="route",
    )(p3)


SC_CORES = 2
SC_SUBCORES = 16
SC_GATHER_ROWS = 32


def _gather(h2, idx_flat):
    t, d = h2.shape
    nrows = idx_flat.shape[0]
    nc, ns = SC_CORES, SC_SUBCORES
    nw = nc * ns
    bw = SC_GATHER_ROWS
    per_w = nrows // nw
    nb = per_w // bw
    assert nrows % (nw * bw * 2) == 0
    mesh = plsc.VectorSubcoreMesh(core_axis_name="c", subcore_axis_name="s", num_cores=nc, num_subcores=ns)

    @functools.partial(
        pl.kernel, mesh=mesh, out_type=jax.ShapeDtypeStruct((nrows, d), h2.dtype),
        scratch_types=[pltpu.VMEM((nb, bw), I32), pltpu.VMEM((bw, d), h2.dtype), pltpu.VMEM((bw, d), h2.dtype),
                       pltpu.SemaphoreType.DMA, pltpu.SemaphoreType.DMA],
        name="gather")
    def run(table_hbm, idx_hbm, out_hbm, idx_v, rows_a, rows_b, sem_a, sem_b):
        wid = lax.axis_index("s") * nc + lax.axis_index("c")
        base = wid * per_w
        pltpu.sync_copy(idx_hbm.at[wid], idx_v)

        def gather(j, buf, sem):
            return pltpu.make_async_copy(table_hbm.at[idx_v.at[j]], buf, sem)

        gather(0, rows_a, sem_a).start()

        @pl.loop(0, nb, step=2)
        def _(j):
            gather(j + 1, rows_b, sem_b).start()
            gather(j, rows_a, sem_a).wait()
            pltpu.sync_copy(rows_a, out_hbm.at[pl.ds(base + j * bw, bw)])

            @pl.when(j + 2 < nb)
            def _():
                gather(j + 2, rows_a, sem_a).start()

            gather(j + 1, rows_b, sem_b).wait()
            pltpu.sync_copy(rows_b, out_hbm.at[pl.ds(base + (j + 1) * bw, bw)])

    return run(h2, idx_flat.reshape(nw, nb, bw))


def _ffn_kernel(xe_ref, aff_ref, wg_ref, wu_ref, wd_ref, out_ref, xb_ref, hid_ref, *, fc):
    xb_ref[...] = xe_ref[...].astype(BF16)
    for c in range(wg_ref.shape[1] // fc):
        g = _dot(xb_ref[...], wg_ref[:, c * fc:(c + 1) * fc])
        u = _dot(xb_ref[...], wu_ref[:, c * fc:(c + 1) * fc])
        hid_ref[:, c * fc:(c + 1) * fc] = (g * _sigmoid(g) * u).astype(BF16)
    y = _dot(hid_ref[...], wd_ref[...])
    for j in range(aff_ref.shape[0]):
        a = jnp.broadcast_to(aff_ref[j:j + 1, :], (LANES, LANES)).T[:, 0:1]
        out_ref[j * LANES:(j + 1) * LANES, :] = (y[j * LANES:(j + 1) * LANES, :] * a).astype(out_ref.dtype)


def _ffn(xe, aff2d, w_gate, w_up, w_down, cap, rs, fc):
    nrows, d = xe.shape
    e, _, ff = w_gate.shape
    rpe = cap // rs
    return pl.pallas_call(
        functools.partial(_ffn_kernel, fc=fc),
        grid=(e, rpe),
        in_specs=[
            pl.BlockSpec((rs, d), lambda ei, r: (ei * rpe + r, 0)),
            pl.BlockSpec((None, rs // LANES, LANES), lambda ei, r: (ei * rpe + r, 0, 0)),
            pl.BlockSpec((None, d, ff), lambda ei, r: (ei, 0, 0)),
            pl.BlockSpec((None, d, ff), lambda ei, r: (ei, 0, 0)),
            pl.BlockSpec((None, ff, d), lambda ei, r: (ei, 0, 0)),
        ],
        out_specs=pl.BlockSpec((rs, d), lambda ei, r: (ei * rpe + r, 0)),
        out_shape=jax.ShapeDtypeStruct((nrows, d), BF16),
        scratch_shapes=[pltpu.VMEM((rs, d), BF16), pltpu.VMEM((rs, ff), BF16)],
        compiler_params=_cparams(("parallel", "arbitrary")),
        name="expert_ffn",
    )(xe, aff2d.reshape(nrows // rs, rs // LANES, LANES), w_gate, w_up, w_down)


def _combine_kernel(off_ref, tot_ref, code_ref, x1_ref, fw_ref, ye_hbm, out_ref, buf_ref, sem,
                    *, cap, n_experts, qc, final_norm):
    blk = pl.program_id(0)
    nblk = pl.num_programs(0)
    tb = x1_ref.shape[0]

    ch = BF16_ROWS

    rpb = code_ref.shape[1]

    def layout(b):
        qbase = jnp.int32(0)
        chunks, shifts = [], []
        for e in range(n_experts):
            s = off_ref[e, b * rpb]
            c_lo = s // ch
            first_row = qbase + (s - c_lo * ch)
            n = jnp.int32(0)
            per_row = []
            for h in range(rpb):
                per_row.append(first_row + n)
                n = n + tot_ref[e, b * rpb + h]
            nch = jnp.where(n > 0, (s + n - 1) // ch - c_lo + 1, 0)
            chunks.append((c_lo, nch, qbase))
            shifts.append(per_row)
            qbase = qbase + nch * ch
        return chunks, shifts, qbase

    def chunk_copy(src, dst, slot):
        return pltpu.make_async_copy(ye_hbm.at[pl.ds(src, ch), :],
                                     buf_ref.at[slot, pl.ds(dst, ch), :], sem.at[slot])

    def fetch(b, slot):
        chunks, _, _ = layout(b)
        for e, (c_lo, nch, qbase) in enumerate(chunks):
            def issue(i, carry, e=e, c_lo=c_lo, qbase=qbase):
                src = pl.multiple_of(e * cap + (c_lo + i) * ch, ch)
                dst = pl.multiple_of(qbase + i * ch, ch)
                chunk_copy(src, dst, slot).start()
                return carry

            lax.fori_loop(0, nch, issue, 0)

    slot = blk % 2

    @pl.when(blk == 0)
    def _():
        buf_ref[...] = jnp.zeros_like(buf_ref)
        fetch(blk, slot)

    @pl.when(blk + 1 < nblk)
    def _():
        fetch(blk + 1, 1 - slot)

    _, shifts, qtot = layout(blk)

    def drain(i, carry):
        chunk_copy(0, 0, slot).wait()
        return carry

    lax.fori_loop(0, qtot // ch, drain, 0)

    nq = (qtot + qc - 1) // qc
    tgt = []
    for h in range(rpb):
        code_t = code_ref[:, h, 0, :].astype(F32).T
        per_e = []
        for e in range(n_experts):
            ce = code_t[:, e:e + 1].astype(I32)
            per_e.append(jnp.broadcast_to(jnp.where(ce >= 0, ce + shifts[e][h], -1), (LANES, LANES)))
        tgt.append(per_e)
    lane = _iota((LANES, LANES), 1)

    def seg_sum(k, acc):
        q0 = pl.multiple_of(k * qc, qc)
        cols = []
        for c in range(qc // LANES):
            qi = q0 + c * LANES + lane
            parts = []
            for h in range(rpb):
                s_col = jnp.zeros((LANES, LANES), F32)
                for e in range(n_experts):
                    s_col = jnp.where(tgt[h][e] == qi, 1.0, s_col)
                parts.append(s_col.astype(BF16))
            cols.append(jnp.concatenate(parts, axis=0))
        return acc + _dot(jnp.concatenate(cols, axis=1), buf_ref[slot, pl.ds(q0, qc), :])

    ffn = lax.fori_loop(0, nq, seg_sum, jnp.zeros(x1_ref.shape, F32))
    x2 = x1_ref[...] + ffn
    if final_norm:
        x2 = x2 * lax.rsqrt(jnp.mean(x2 * x2, axis=-1, keepdims=True) + EPS) * fw_ref[...]
    out_ref[...] = x2


def _combine(off, tot, code, x1, ye, final_w, cap, final_norm):
    t, d = x1.shape
    e = code.shape[0]
    rpb = _pick(t // LANES, (2, 1))
    tb = rpb * LANES
    qc = 256
    max_rows = e * (tb + 2 * BF16_ROWS)
    buf_rows = ((max_rows + qc - 1) // qc) * qc
    return pl.pallas_call(
        functools.partial(_combine_kernel, cap=cap, n_experts=e, qc=qc, final_norm=final_norm),
        grid_spec=pltpu.PrefetchScalarGridSpec(
            num_scalar_prefetch=2,
            grid=(t // tb,),
            in_specs=[
                pl.BlockSpec((e, rpb, 1, LANES), lambda i, *_: (0, i, 0, 0)),
                pl.BlockSpec((tb, d), lambda i, *_: (i, 0)),
                pl.BlockSpec((1, d), lambda i, *_: (0, 0)),
                pl.BlockSpec(memory_space=pl.ANY),
            ],
            out_specs=pl.BlockSpec((tb, d), lambda i, *_: (i, 0)),
            scratch_shapes=[pltpu.VMEM((2, buf_rows, d), ye.dtype), pltpu.SemaphoreType.DMA((2,))],
        ),
        out_shape=jax.ShapeDtypeStruct((t, d), F32),
        compiler_params=_cparams(("arbitrary",)),
        name="combine",
    )(off, tot, code.reshape(e, t // LANES, 1, LANES), x1, final_w, ye)


def _pick(n, prefs):
    for p in prefs:
        if n % p == 0:
            return p
    raise ValueError(f"no tile size for {n}")


def _layer_front(x, lw):
    batch, seq, d = x.shape
    t = batch * seq
    heads = lw["dt_bias_f"].shape[-1]
    d_inner = lw["ssm_norm_w"].shape[-1]
    conv_ch = lw["conv_w"].shape[-1]
    n_experts = lw["w_router"].shape[-1]
    cap = CAPACITY_FACTOR * t // n_experts
    assert seq % CHUNK == 0 and cap % LANES == 0 and 2 * heads <= LANES and CHUNK == LANES

    x2d = x.reshape(t, d)
    main, acum, src_t, wrow_t = _in_proj(
        x2d, lw["norm_mix_w"], lw["w_main"], lw["w_dt"], lw["dt_bias_all"], lw["a_log_all"], heads,
        _pick(t, (512, 256, 128)), _pick(lw["w_main"].shape[1], (1024, 512)))
    xc = _conv(main, lw["conv_w"], lw["conv_b"], seq, d_inner,
               _pick(seq, (1024, 512, 256, 128)), _pick(conv_ch, (1024, 512)))
    cpb = _pick(seq // CHUNK, (4, 2, 1))
    yf = _ssd(xc, acum, src_t, wrow_t, lw["d_skip_exp"], batch, seq, d_inner, heads, False, cpb)
    yb = _ssd(xc, acum, src_t, wrow_t, lw["d_skip_exp"], batch, seq, d_inner, heads, True, cpb)
    x1, h2, probs_t = _mix(yf, yb, main, x2d, lw["ssm_norm_w"], lw["w_ssm_out"], lw["w_pool"],
                           lw["pool_scale"], lw["w_o"], lw["norm_ffn_w"], lw["w_router_pad"],
                           seq, d_inner, conv_ch, n_experts, _pick(seq, (512, 256, 128)))
    idx, aff, code, off, tot = _route(probs_t, cap)
    xe = _gather(h2, idx.reshape(n_experts * cap))
    return dict(shape=x.shape, cap=cap, x1=x1, xe=xe, aff=aff, code=code, off=off, tot=tot)


def _layer_back(fr, lw, final_w, final_norm):
    batch, seq, d = fr["shape"]
    t = batch * seq
    cap = fr["cap"]
    n_experts = fr["code"].shape[0]
    ye = _ffn(fr["xe"], fr["aff"].reshape(n_experts * cap // LANES, LANES), lw["w_gate"], lw["w_up"],
              lw["w_down"], cap, _pick(cap, (512, 256, 128)), _pick(lw["w_gate"].shape[-1], (256, 128)))
    r = t // LANES
    out = _combine(fr["off"].reshape(n_experts, r), fr["tot"].reshape(n_experts, r), fr["code"], fr["x1"],
                   ye, final_w, cap, final_norm)
    return out.reshape(batch, seq, d)


def _prep_layer(i, norm_mix_w, w_in, conv_w, conv_b, dt_bias_f, dt_bias_b, a_log_f, a_log_b, d_skip,
                ssm_norm_w, w_ssm_out, w_pool, pool_scale, w_o, norm_ffn_w, w_router, w_gate, w_up, w_down):
    d = w_in.shape[1]
    heads = dt_bias_f.shape[-1]
    d_inner = ssm_norm_w.shape[-1]
    conv_ch = conv_w.shape[-1]
    p = d_inner // heads
    o2 = d_inner + conv_ch
    o4 = o2 + 2 * heads
    wi = w_in[i]
    w_main = jnp.concatenate([wi[:, :o2], wi[:, o4:]], axis=1).astype(BF16)
    w_dt = jnp.pad(wi[:, o2:o4], ((0, 0), (0, LANES - 2 * heads)))
    row = lambda v: v.reshape(1, -1).astype(F32)
    padl = lambda v: jnp.pad(v.reshape(1, -1).astype(F32), ((0, 0), (0, LANES - v.size)))
    return {
        "norm_mix_w": row(norm_mix_w[i]), "w_main": w_main, "w_dt": w_dt,
        "conv_w": conv_w[i], "conv_b": row(conv_b[i]),
        "dt_bias_f": dt_bias_f[i],
        "dt_bias_all": padl(jnp.concatenate([dt_bias_f[i], dt_bias_b[i]])),
        "a_log_all": padl(jnp.concatenate([a_log_f[i], a_log_b[i]])),
        "d_skip_exp": row(jnp.repeat(d_skip[i], p)),
        "ssm_norm_w": row(ssm_norm_w[i]), "w_ssm_out": w_ssm_out[i].astype(BF16),
        "w_pool": w_pool[i].astype(BF16), "pool_scale": row(pool_scale[i]),
        "w_o": w_o[i].astype(BF16), "norm_ffn_w": row(norm_ffn_w[i]),
        "w_router": w_router[i],
        "w_router_pad": jnp.pad(w_router[i].astype(F32), ((0, 0), (0, LANES - w_router.shape[-1]))),
        "w_gate": w_gate[i].astype(BF16), "w_up": w_up[i].astype(BF16), "w_down": w_down[i].astype(BF16),
    }


def kernel(x_prompt, x_sample, norm_mix_w, w_in, conv_w, conv_b, dt_bias_f, dt_bias_b, a_log_f, a_log_b,
           d_skip, ssm_norm_w, w_ssm_out, w_pool, pool_scale, w_o, norm_ffn_w, w_router, w_gate, w_up,
           w_down, norm_final_w):
    depth = w_in.shape[0]
    layers = [_prep_layer(i, norm_mix_w, w_in, conv_w, conv_b, dt_bias_f, dt_bias_b, a_log_f, a_log_b,
                          d_skip, ssm_norm_w, w_ssm_out, w_pool, pool_scale, w_o, norm_ffn_w, w_router,
                          w_gate, w_up, w_down) for i in range(depth)]
    final_w = norm_final_w.reshape(1, -1).astype(F32)

    xs = [x_prompt, x_sample]
    for i, lw in enumerate(layers):
        fronts = [_layer_front(x, lw) for x in xs]
        xs = [_layer_back(fr, lw, final_w, final_norm=(i == depth - 1)) for fr in fronts]
    return tuple(xs)
```

```python
import functools

import jax
import jax.numpy as jnp
from jax import lax
from jax.experimental import pallas as pl
from jax.experimental.pallas import tpu as pltpu
from jax.experimental.pallas import tpu_sc as plsc

F32 = jnp.float32
BF16 = jnp.bfloat16
I32 = jnp.int32

EPS = 1e-6
LOG2E = 1.4426950408889634
CHUNK = 128
LANES = 128
SUBLANES = 8
BF16_ROWS = 16
SSM_GROUPS = 8
POOL_WINDOWS = (2, 4, 8, 16)
CAPACITY_FACTOR = 2
VMEM_LIMIT = 56 * 1024 * 1024


def _cparams(sem):
    return pltpu.CompilerParams(dimension_semantics=sem, vmem_limit_bytes=VMEM_LIMIT)


def _iota(shape, dim):
    return lax.broadcasted_iota(I32, shape, dim)


def _sigmoid(x):
    return 1.0 / (1.0 + jnp.exp(-x))


def _split_bf16(x, terms):
    parts = []
    rem = x
    for _ in range(terms):
        p = rem.astype(BF16)
        parts.append(p)
        rem = rem - p.astype(F32)
    return parts


def _dot(a, b):
    return jnp.dot(a, b, preferred_element_type=F32)


def _dot_nt(a, b):
    return lax.dot_general(a, b, (((1,), (1,)), ((), ())), preferred_element_type=F32)


def _dot_exact_lhs(mask_bf, x, terms=3):
    out = None
    for p in _split_bf16(x, terms):
        d = _dot(mask_bf, p)
        out = d if out is None else out + d
    return out


def _in_proj_kernel(x_ref, nw_ref, w_ref, wdt_ref, bias_ref, alog_ref, main_ref, acum_ref, src_ref, wrow_ref,
                    h_ref, *, tn, heads, silu_end, sigmoid_start):
    x = x_ref[...]
    h = x * lax.rsqrt(jnp.mean(x * x, axis=-1, keepdims=True) + EPS) * nw_ref[...]
    hb, hl = _split_bf16(h, 2)
    h_ref[...] = hb
    wb, wl = _split_bf16(wdt_ref[...], 2)
    v = _dot(hb, wb) + _dot(hl, wb) + _dot(hb, wl) + bias_ref[...]
    dt = jnp.maximum(v, 0.0) + jnp.log1p(jnp.exp(-jnp.abs(v)))

    L = CHUNK
    neg_a = -jnp.exp(alog_ref[...])
    row = _iota((L, L), 0)
    col = _iota((L, L), 1)
    tril = jnp.where(col <= row, 1.0, 0.0).astype(BF16)
    triu = jnp.where(col >= row, 1.0, 0.0).astype(BF16)
    fwd_col = col < heads
    fwd_row = _iota((LANES, 1), 0) < heads
    for k in range(x_ref.shape[0] // L):
        rows = slice(k * L, (k + 1) * L)
        dtk = dt[rows, :]
        da = dtk * neg_a
        acum = jnp.where(fwd_col, _dot_exact_lhs(tril, da), _dot_exact_lhs(triu, da))
        acum_t = acum.T
        a_last_t = jnp.where(fwd_row, acum_t[:, L - 1:L], acum_t[:, 0:1])
        src_t = acum_t - jnp.log(dtk).T
        acum_ref[rows, :] = acum * LOG2E
        src_ref[rows, :] = src_t * LOG2E
        wrow_ref[rows, :] = jnp.exp(a_last_t - src_t)

    for j in range(w_ref.shape[1] // tn):
        o = _dot(h_ref[...], w_ref[:, j * tn:(j + 1) * tn])
        if (j + 1) * tn <= silu_end:
            o = o * _sigmoid(o)
        elif j * tn >= sigmoid_start:
            o = _sigmoid(o)
        else:
            assert j * tn >= silu_end and (j + 1) * tn <= sigmoid_start
        main_ref[:, j * tn:(j + 1) * tn] = o.astype(main_ref.dtype)


def _in_proj(x2d, norm_w, w_main, w_dt, dt_bias, alog_all, heads, silu_end, sigmoid_start, tm, tn):
    t, d = x2d.shape
    ncols = w_main.shape[1]
    const = lambda shape: pl.BlockSpec(shape, lambda i: (0, 0), pipeline_mode=pl.Buffered(1))
    table = pl.BlockSpec((tm, LANES), lambda i: (i, 0))
    return pl.pallas_call(
        functools.partial(_in_proj_kernel, tn=tn, heads=heads, silu_end=silu_end, sigmoid_start=sigmoid_start),
        grid=(t // tm,),
        in_specs=[
            pl.BlockSpec((tm, d), lambda i: (i, 0)),
            const((1, d)),
            const((d, ncols)),
            const((d, LANES)),
            const((1, LANES)),
            const((1, LANES)),
        ],
        out_specs=[pl.BlockSpec((tm, ncols), lambda i: (i, 0)), table, table, table],
        out_shape=[jax.ShapeDtypeStruct((t, ncols), BF16)] + [jax.ShapeDtypeStruct((t, LANES), F32)] * 3,
        scratch_shapes=[pltpu.VMEM((tm, d), BF16)],
        compiler_params=_cparams(("parallel",)),
        name="in_proj",
    )(x2d, norm_w, w_main, w_dt, dt_bias, alog_all)


def _conv_kernel(cur_ref, prev_ref, next_ref, w_ref, b_ref, out_ref, buf_ref, *, tiles_per_seq):
    i = pl.program_id(0)
    tr = cur_ref.shape[0]
    halo = prev_ref.shape[0]
    pos = i % tiles_per_seq
    keep_prev = jnp.where(pos == 0, 0.0, 1.0)
    keep_next = jnp.where(pos == tiles_per_seq - 1, 0.0, 1.0)
    buf_ref[0:halo, :] = (prev_ref[...].astype(F32) * keep_prev).astype(BF16)
    buf_ref[halo:halo + tr, :] = cur_ref[...]
    buf_ref[halo + tr:halo + tr + halo, :] = (next_ref[...].astype(F32) * keep_next).astype(BF16)
    k = w_ref.shape[0]
    pad = k // 2
    win = CHUNK + 2 * halo
    ti = _iota((CHUNK, win), 0)
    ji = _iota((CHUNK, win), 1)
    shift = {j: jnp.where(ji == ti + halo + (j - pad), 1.0, 0.0).astype(BF16) for j in range(k) if j != pad}
    for b in range(tr // CHUNK):
        xw = buf_ref[b * CHUNK:b * CHUNK + win, :]
        acc = b_ref[...] + w_ref[pad:pad + 1, :] * cur_ref[b * CHUNK:(b + 1) * CHUNK, :].astype(F32)
        for j in shift:
            acc = acc + w_ref[j:j + 1, :] * _dot(shift[j], xw)
        out_ref[b * CHUNK:(b + 1) * CHUNK, :] = (acc * _sigmoid(acc)).astype(out_ref.dtype)


def _conv(main, conv_w, conv_b, seq, col0, tr, tc):
    t = main.shape[0]
    k, ch = conv_w.shape
    halo = BF16_ROWS
    cb0 = col0 // tc
    nhalo = t // halo
    rh = tr // halo
    return pl.pallas_call(
        functools.partial(_conv_kernel, tiles_per_seq=seq // tr),
        grid=(t // tr, ch // tc),
        in_specs=[
            pl.BlockSpec((tr, tc), lambda i, j: (i, cb0 + j)),
            pl.BlockSpec((halo, tc), lambda i, j: (jnp.maximum(i * rh - 1, 0), cb0 + j)),
            pl.BlockSpec((halo, tc), lambda i, j: (jnp.minimum((i + 1) * rh, nhalo - 1), cb0 + j)),
            pl.BlockSpec((k, tc), lambda i, j: (0, j)),
            pl.BlockSpec((1, tc), lambda i, j: (0, j)),
        ],
        out_specs=pl.BlockSpec((tr, tc), lambda i, j: (i, j)),
        out_shape=jax.ShapeDtypeStruct((t, ch), BF16),
        scratch_shapes=[pltpu.VMEM((tr + 2 * halo, tc), BF16)],
        compiler_params=_cparams(("parallel", "parallel")),
        name="conv_silu",
    )(main, main, main, conv_w, conv_b)


def _ssd_kernel(xs_ref, b_ref, c_ref, acum_ref, src_ref, wrow_ref, dskip_ref, y_ref, state_ref,
                *, reverse, heads, add_skip):
    @pl.when(pl.program_id(1) == 0)
    def _():
        state_ref[...] = jnp.zeros_like(state_ref)

    L = CHUNK
    cpb = xs_ref.shape[0] // L
    d_inner = xs_ref.shape[1]
    p = d_inner // heads
    hpg = heads // SSM_GROUPS
    n = b_ref.shape[1] // SSM_GROUPS
    assert n == L
    assert hpg % 2 == 0
    off = heads if reverse else 0
    first = _iota((L, 2 * p), 1) < p

    row = _iota((L, L), 0)
    col = _iota((L, L), 1)
    mask = (col >= row) if reverse else (col <= row)
    last = 0 if reverse else L - 1

    def chunk(k, carry):
        r0 = pl.multiple_of(((cpb - 1 - k) if reverse else k) * L, L)
        rows = pl.ds(r0, L)
        acum = acum_ref[rows, :]
        src_t = src_ref[rows, :]
        wrow_t = wrow_ref[rows, :]
        dchunk = jnp.exp2(acum[last:last + 1, :])

        for g in range(SSM_GROUPS):
            bg = b_ref[rows, g * n:(g + 1) * n]
            cg = c_ref[rows, g * n:(g + 1) * n]
            cgf = cg.astype(F32)
            cb = _dot_nt(cg, bg)
            btf = bg.astype(F32).T
            for j in range(0, hpg, 2):
                hh0 = g * hpg + j
                cols = slice(hh0 * p, (hh0 + 2) * p)
                xpair = xs_ref[rows, cols]
                st = state_ref[hh0 // 2]
                rhs = jnp.concatenate([xpair, st.astype(BF16)], axis=0)
                ys, ds, dch = [], [], []
                for hh in (hh0, hh0 + 1):
                    h = off + hh
                    colb = jnp.broadcast_to(acum[:, h:h + 1], (L, L))
                    m = (cb * jnp.exp2(jnp.where(mask, colb - src_t[h:h + 1, :], -1e30))).astype(BF16)
                    cs = (cgf * jnp.exp2(colb)).astype(BF16)
                    ys.append(_dot(jnp.concatenate([m, cs], axis=1), rhs))
                    bts = (btf * wrow_t[h:h + 1, :]).astype(BF16)
                    ds.append(_dot(bts, xpair))
                    dch.append(dchunk[:, h:h + 1])
                y = jnp.where(first, ys[0], ys[1])
                state_ref[hh0 // 2] = (st * jnp.where(first[0:1, :], dch[0], dch[1])
                                       + jnp.where(first, ds[0], ds[1]))
                if add_skip:
                    y = y + xpair.astype(F32) * dskip_ref[:, cols]
                y_ref[rows, cols] = y.astype(y_ref.dtype)
        return carry

    lax.fori_loop(0, cpb, chunk, 0)


def _ssd(xc, acum, src_t, wrow_t, dskip_exp, batch, seq, d_inner, heads, reverse, cpb):
    t = xc.shape[0]
    nb = seq // (cpb * CHUNK)
    rb = cpb * CHUNK
    gn = (xc.shape[1] - d_inner) // 2
    bblk = d_inner // gn
    n = gn // SSM_GROUPS

    def rowblk(b, c):
        return b * nb + (nb - 1 - c if reverse else c)

    return pl.pallas_call(
        functools.partial(_ssd_kernel, reverse=reverse, heads=heads, add_skip=not reverse),
        grid=(batch, nb),
        in_specs=[
            pl.BlockSpec((rb, d_inner), lambda b, c: (rowblk(b, c), 0)),
            pl.BlockSpec((rb, gn), lambda b, c: (rowblk(b, c), bblk)),
            pl.BlockSpec((rb, gn), lambda b, c: (rowblk(b, c), bblk + 1)),
            pl.BlockSpec((rb, LANES), lambda b, c: (rowblk(b, c), 0)),
            pl.BlockSpec((rb, LANES), lambda b, c: (rowblk(b, c), 0)),
            pl.BlockSpec((rb, LANES), lambda b, c: (rowblk(b, c), 0)),
            pl.BlockSpec((1, d_inner), lambda b, c: (0, 0)),
        ],
        out_specs=pl.BlockSpec((rb, d_inner), lambda b, c: (rowblk(b, c), 0)),
        out_shape=jax.ShapeDtypeStruct((t, d_inner), BF16),
        scratch_shapes=[pltpu.VMEM((heads // 2, n, 2 * d_inner // heads), F32)],
        compiler_params=_cparams(("parallel", "arbitrary")),
        name="ssd_bwd" if reverse else "ssd_fwd",
    )(xc, xc, xc, acum, src_t, wrow_t, dskip_exp)


def _mix_kernel(yf_ref, yb_ref, z_ref, u_ref, up_ref, un_ref, ga_ref, gb_ref, x_ref,
                snw_ref, wso_ref, wp_ref, ps_ref, wo_ref, fnw_ref, wr_ref,
                x1_ref, h2_ref, pt_ref, *, tiles_per_seq, seq, n_experts):
    tm = x_ref.shape[0]
    d_inner = z_ref.shape[1]
    gw = d_inner // SSM_GROUPS
    halo = up_ref.shape[0]
    sub = min(tm, 2 * CHUNK)
    pos0 = (pl.program_id(0) % tiles_per_seq) * tm
    pw = u_ref.shape[1] // len(POOL_WINDOWS)
    keep_prev = jnp.where(pos0 > 0, 1.0, 0.0)
    keep_next = jnp.where(pos0 + tm < seq, 1.0, 0.0)
    wb, wl = _split_bf16(wr_ref[...], 2)

    tq = _iota((sub, sub), 0)
    sq = _iota((sub, sub), 1)
    tp = _iota((sub, halo), 0)
    jp = _iota((sub, halo), 1)
    bands = []
    for win in POOL_WINDOWS:
        lo = win // 2
        hi = win - lo
        in_c = (sq >= tq - lo) & (sq < tq + hi)
        in_p = jp - halo >= tp - lo
        in_n = jp + sub < tp + hi
        bands.append((jnp.where(in_c, 1.0, 0.0).astype(BF16), in_p, in_n))

    for si in range(tm // sub):
        rows = slice(si * sub, (si + 1) * sub)
        at_top = si == 0
        at_bottom = si == tm // sub - 1

        y = (yf_ref[rows, :].astype(F32) + yb_ref[rows, :].astype(F32)) * z_ref[rows, :].astype(F32)
        parts = []
        for g in range(SSM_GROUPS):
            blk = y[:, g * gw:(g + 1) * gw]
            sc = lax.rsqrt(jnp.mean(blk * blk, axis=-1, keepdims=True) + EPS)
            parts.append((blk * sc * snw_ref[:, g * gw:(g + 1) * gw]).astype(BF16))
        y_a = _dot(jnp.concatenate(parts, axis=1), wso_ref[...])

        tpos = pos0 + si * sub + _iota((sub, 1), 0)
        pooled = []
        for gi, win in enumerate(POOL_WINDOWS):
            lo = win // 2
            hi = win - lo
            band_c, in_p, in_n = bands[gi]
            cg = slice(gi * pw, (gi + 1) * pw)
            ug = u_ref[rows, cg]
            u_prev = up_ref[:, cg] if at_top else u_ref[si * sub - halo:si * sub, cg]
            u_next = un_ref[:, cg] if at_bottom else u_ref[(si + 1) * sub:(si + 1) * sub + halo, cg]
            band_p = jnp.where(in_p, keep_prev if at_top else 1.0, 0.0).astype(BF16)
            band_n = jnp.where(in_n, keep_next if at_bottom else 1.0, 0.0).astype(BF16)
            s = _dot(band_c, ug) + _dot(band_p, u_prev) + _dot(band_n, u_next)
            cnt = (jnp.minimum(tpos + hi, seq) - jnp.maximum(tpos - lo, 0)).astype(F32)
            pg = (s / cnt - ug.astype(F32)).astype(BF16)
            pooled.append(_dot(pg, wp_ref[gi]))
        y_b = jnp.concatenate(pooled, axis=1) * ps_ref[...]

        merged = ga_ref[rows, :].astype(F32) * y_a + gb_ref[rows, :].astype(F32) * y_b
        x1 = x_ref[rows, :] + _dot(merged.astype(BF16), wo_ref[...])
        x1_ref[rows, :] = x1

        h2 = x1 * lax.rsqrt(jnp.mean(x1 * x1, axis=-1, keepdims=True) + EPS) * fnw_ref[...]
        h2_ref[rows, :] = h2
        hb, hl = _split_bf16(h2, 2)
        logits = _dot(hb, wb) + _dot(hl, wb) + _dot(hb, wl)
        logits = jnp.where(_iota(logits.shape, 1) < n_experts, logits, -1e30)
        e = jnp.exp(logits - jnp.max(logits, axis=-1, keepdims=True))
        probs = e / jnp.sum(e, axis=-1, keepdims=True)
        pt_ref[:, rows] = probs.T[:n_experts, :]


def _mix(yf, yb, main, x2d, ssm_norm_w, w_ssm_out, w_pool, pool_scale, w_o, norm_ffn_w, w_router_pad,
         seq, d_inner, conv_ch, n_experts, tm):
    t, d = x2d.shape
    halo = BF16_ROWS
    rh = tm // halo
    nhalo = t // halo
    zb = 0
    ub = (d_inner + conv_ch) // d
    full = lambda shape: pl.BlockSpec(shape, lambda i: (0,) * len(shape))
    return pl.pallas_call(
        functools.partial(_mix_kernel, tiles_per_seq=seq // tm, seq=seq, n_experts=n_experts),
        grid=(t // tm,),
        in_specs=[
            pl.BlockSpec((tm, d_inner), lambda i: (i, 0)),
            pl.BlockSpec((tm, d_inner), lambda i: (i, 0)),
            pl.BlockSpec((tm, d_inner), lambda i: (i, zb)),
            pl.BlockSpec((tm, d), lambda i: (i, ub)),
            pl.BlockSpec((halo, d), lambda i: (jnp.maximum(i * rh - 1, 0), ub)),
            pl.BlockSpec((halo, d), lambda i: (jnp.minimum((i + 1) * rh, nhalo - 1), ub)),
            pl.BlockSpec((tm, d), lambda i: (i, ub + 1)),
            pl.BlockSpec((tm, d), lambda i: (i, ub + 2)),
            pl.BlockSpec((tm, d), lambda i: (i, 0)),
            full((1, d_inner)),
            full((d_inner, d)),
            full(w_pool.shape),
            full((1, d)),
            full((d, d)),
            full((1, d)),
            full((d, LANES)),
        ],
        out_specs=[
            pl.BlockSpec((tm, d), lambda i: (i, 0)),
            pl.BlockSpec((tm, d), lambda i: (i, 0)),
            pl.BlockSpec((n_experts, tm), lambda i: (0, i)),
        ],
        out_shape=[
            jax.ShapeDtypeStruct((t, d), F32),
            jax.ShapeDtypeStruct((t, d), F32),
            jax.ShapeDtypeStruct((n_experts, t), F32),
        ],
        compiler_params=_cparams(("parallel",)),
        name="mix",
    )(yf, yb, main, main, main, main, main, main, x2d,
      ssm_norm_w, w_ssm_out, w_pool, pool_scale, w_o, norm_ffn_w, w_router_pad)


def _route_kernel(p_ref, idx_ref, aff_ref, code_ref, off_ref, tot_ref, *, cap, ct):
    p = p_ref[0]
    r = p.shape[0]
    key = pltpu.bitcast(p, I32)

    def search(i, prefix):
        cand = prefix | lax.shift_left(jnp.int32(1), 30 - i)
        cnt = jnp.sum(jnp.where(key >= cand, 1, 0))
        return jnp.where(cnt >= cap, cand, prefix)

    tau = lax.fori_loop(0, 31, search, jnp.int32(0))

    lane_r = _iota((LANES, LANES), 0)
    lane_c = _iota((LANES, LANES), 1)
    tri_incl = jnp.where(lane_r <= lane_c, 1.0, 0.0).astype(BF16)
    rr = _iota((r, r), 0)
    rc = _iota((r, r), 1)
    tril_strict = jnp.where(rc < rr, 1.0, 0.0).astype(BF16)

    def cumsum(mask):
        mb = jnp.where(mask, 1.0, 0.0).astype(BF16)
        incl = _dot(mb, tri_incl)
        tot = jnp.broadcast_to(incl[:, LANES - 1:LANES], (r, LANES))
        off = _dot(tril_strict, tot.astype(BF16))
        return mb, incl, off, tot

    gt = key > tau
    eq = key == tau
    need = (cap - jnp.sum(jnp.where(gt, 1, 0))).astype(F32)
    mb_eq, incl_eq, off_eq, _ = cumsum(eq)
    rank_eq = off_eq + incl_eq - mb_eq.astype(F32)
    sel = gt | (eq & (rank_eq < need))
    mb, incl, off, tot = cumsum(sel)

    local = incl - mb.astype(F32)
    code_ref[0] = jnp.where(sel, local, -1.0).astype(I32)

    ones8 = jnp.ones((SUBLANES, LANES), BF16)
    tot_t = _dot_nt(ones8, mb)
    triu_strict = jnp.where(rr < rc, 1.0, 0.0).astype(BF16)
    off_t = _dot(tot_t.astype(BF16), triu_strict)
    off_ref[0] = off_t[0:1, :].astype(I32)
    tot_ref[0] = tot_t[0:1, :].astype(I32)

    incl_t = _dot_nt(jnp.where(lane_c <= lane_r, 1.0, 0.0).astype(BF16), mb).astype(BF16)
    p_t = _split_bf16(p.T, 3)
    incl_row = (off + tot)[:, 0:1]
    off_col = off[:, 0:1]
    for c0 in range(0, cap, ct):
        cvec = (c0 + _iota((1, ct), 1)).astype(F32)
        before = incl_row <= cvec
        rsel = jnp.sum(jnp.where(before, 1, 0), axis=0, keepdims=True)
        onehot = _iota((r, ct), 0) == rsel
        onehot_bf = jnp.where(onehot, 1.0, 0.0).astype(BF16)
        g_t = _dot(incl_t, onehot_bf)
        offc = jnp.sum(jnp.where(onehot, off_col, 0.0), axis=0, keepdims=True)
        target = cvec - offc
        lane = jnp.sum(jnp.where(g_t <= target, 1, 0), axis=0, keepdims=True)
        idx_ref[0, :, c0:c0 + ct] = rsel * LANES + lane
        a_t = _dot(p_t[0], onehot_bf) + _dot(p_t[1], onehot_bf) + _dot(p_t[2], onehot_bf)
        hit = _iota((LANES, ct), 0) == lane
        aff_ref[0, :, c0:c0 + ct] = jnp.sum(jnp.where(hit, a_t, 0.0), axis=0, keepdims=True)


def _route(probs_t, cap):
    e, t = probs_t.shape
    r = t // LANES
    ct = min(cap, 512)
    p3 = probs_t.reshape(e, r, LANES)
    return pl.pallas_call(
        functools.partial(_route_kernel, cap=cap, ct=ct),
        grid=(e,),
        in_specs=[pl.BlockSpec((1, r, LANES), lambda i: (i, 0, 0))],
        out_specs=[
            pl.BlockSpec((1, 1, cap), lambda i: (i, 0, 0)),
            pl.BlockSpec((1, 1, cap), lambda i: (i, 0, 0)),
            pl.BlockSpec((1, r, LANES), lambda i: (i, 0, 0)),
            pl.BlockSpec((1, 1, r), lambda i: (i, 0, 0)),
            pl.BlockSpec((1, 1, r), lambda i: (i, 0, 0)),
        ],
        out_shape=[
            jax.ShapeDtypeStruct((e, 1, cap), I32),
            jax.ShapeDtypeStruct((e, 1, cap), F32),
            jax.ShapeDtypeStruct((e, r, LANES), I32),
            jax.ShapeDtypeStruct((e, 1, r), I32),
            jax.ShapeDtypeStruct((e, 1, r), I32),
        ],
        compiler_params=_cparams(("parallel",)),
        name="route",
    )(p3)


SC_CORES = 2
SC_SUBCORES = 16
SC_GATHER_ROWS = 32


def _gather(h2, idx_flat):
    t, d = h2.shape
    nrows = idx_flat.shape[0]
    nc, ns = SC_CORES, SC_SUBCORES
    nw = nc * ns
    bw = SC_GATHER_ROWS
    per_w = nrows // nw
    nb = per_w // bw
    assert nrows % (nw * bw * 2) == 0
    mesh = plsc.VectorSubcoreMesh(core_axis_name="c", subcore_axis_name="s", num_cores=nc, num_subcores=ns)

    @functools.partial(
        pl.kernel, mesh=mesh, out_type=jax.ShapeDtypeStruct((nrows, d), h2.dtype),
        scratch_types=[pltpu.VMEM((nb, bw), I32), pltpu.VMEM((bw, d), h2.dtype), pltpu.VMEM((bw, d), h2.dtype),
                       pltpu.SemaphoreType.DMA, pltpu.SemaphoreType.DMA],
        name="gather")
    def run(table_hbm, idx_hbm, out_hbm, idx_v, rows_a, rows_b, sem_a, sem_b):
        wid = lax.axis_index("s") * nc + lax.axis_index("c")
        base = wid * per_w
        pltpu.sync_copy(idx_hbm.at[wid], idx_v)

        def gather(j, buf, sem):
            return pltpu.make_async_copy(table_hbm.at[idx_v.at[j]], buf, sem)

        gather(0, rows_a, sem_a).start()

        @pl.loop(0, nb, step=2)
        def _(j):
            gather(j + 1, rows_b, sem_b).start()
            gather(j, rows_a, sem_a).wait()
            pltpu.sync_copy(rows_a, out_hbm.at[pl.ds(base + j * bw, bw)])

            @pl.when(j + 2 < nb)
            def _():
                gather(j + 2, rows_a, sem_a).start()

            gather(j + 1, rows_b, sem_b).wait()
            pltpu.sync_copy(rows_b, out_hbm.at[pl.ds(base + (j + 1) * bw, bw)])

    return run(h2, idx_flat.reshape(nw, nb, bw))


def _ffn_kernel(xe_ref, aff_ref, wg_ref, wu_ref, wd_ref, out_ref, xb_ref, hid_ref, *, fc):
    xb_ref[...] = xe_ref[...].astype(BF16)
    for c in range(wg_ref.shape[1] // fc):
        g = _dot(xb_ref[...], wg_ref[:, c * fc:(c + 1) * fc])
        u = _dot(xb_ref[...], wu_ref[:, c * fc:(c + 1) * fc])
        hid_ref[:, c * fc:(c + 1) * fc] = (g * _sigmoid(g) * u).astype(BF16)
    y = _dot(hid_ref[...], wd_ref[...])
    for j in range(aff_ref.shape[0]):
        a = jnp.broadcast_to(aff_ref[j:j + 1, :], (LANES, LANES)).T[:, 0:1]
        out_ref[j * LANES:(j + 1) * LANES, :] = (y[j * LANES:(j + 1) * LANES, :] * a).astype(out_ref.dtype)


def _ffn(xe, aff2d, w_gate, w_up, w_down, cap, rs, fc):
    nrows, d = xe.shape
    e, _, ff = w_gate.shape
    rpe = cap // rs
    return pl.pallas_call(
        functools.partial(_ffn_kernel, fc=fc),
        grid=(e, rpe),
        in_specs=[
            pl.BlockSpec((rs, d), lambda ei, r: (ei * rpe + r, 0)),
            pl.BlockSpec((None, rs // LANES, LANES), lambda ei, r: (ei * rpe + r, 0, 0)),
            pl.BlockSpec((None, d, ff), lambda ei, r: (ei, 0, 0)),
            pl.BlockSpec((None, d, ff), lambda ei, r: (ei, 0, 0)),
            pl.BlockSpec((None, ff, d), lambda ei, r: (ei, 0, 0)),
        ],
        out_specs=pl.BlockSpec((rs, d), lambda ei, r: (ei * rpe + r, 0)),
        out_shape=jax.ShapeDtypeStruct((nrows, d), BF16),
        scratch_shapes=[pltpu.VMEM((rs, d), BF16), pltpu.VMEM((rs, ff), BF16)],
        compiler_params=_cparams(("parallel", "arbitrary")),
        name="expert_ffn",
    )(xe, aff2d.reshape(nrows // rs, rs // LANES, LANES), w_gate, w_up, w_down)


def _combine_kernel(off_ref, tot_ref, code_ref, x1_ref, fw_ref, ye_hbm, out_ref, buf_ref, sel_ref, sem,
                    *, cap, n_experts, qc, final_norm):
    blk = pl.program_id(0)
    nblk = pl.num_programs(0)
    tb = x1_ref.shape[0]

    ch = BF16_ROWS

    rpb = code_ref.shape[1]

    def layout(b):
        qbase = jnp.int32(0)
        chunks, shifts = [], []
        for e in range(n_experts):
            s = off_ref[e, b * rpb]
            c_lo = s // ch
            first_row = qbase + (s - c_lo * ch)
            n = jnp.int32(0)
            per_row = []
            for h in range(rpb):
                per_row.append(first_row + n)
                n = n + tot_ref[e, b * rpb + h]
            nch = jnp.where(n > 0, (s + n - 1) // ch - c_lo + 1, 0)
            chunks.append((c_lo, nch, qbase))
            shifts.append(per_row)
            qbase = qbase + nch * ch
        return chunks, shifts, qbase

    def chunk_copy(src, dst, slot):
        return pltpu.make_async_copy(ye_hbm.at[pl.ds(src, ch), :],
                                     buf_ref.at[slot, pl.ds(dst, ch), :], sem.at[slot])

    def fetch(b, slot):
        chunks, _, _ = layout(b)
        for e, (c_lo, nch, qbase) in enumerate(chunks):
            def issue(i, carry, e=e, c_lo=c_lo, qbase=qbase):
                src = pl.multiple_of(e * cap + (c_lo + i) * ch, ch)
                dst = pl.multiple_of(qbase + i * ch, ch)
                chunk_copy(src, dst, slot).start()
                return carry

            lax.fori_loop(0, nch, issue, 0)

    slot = blk % 2

    @pl.when(blk == 0)
    def _():
        buf_ref[...] = jnp.zeros_like(buf_ref)
        fetch(blk, slot)

    @pl.when(blk + 1 < nblk)
    def _():
        fetch(blk + 1, 1 - slot)

    chunks, shifts, qtot = layout(blk)
    nq = (qtot + qc - 1) // qc

    sel_ref[pl.ds(pl.multiple_of(jnp.maximum(nq - 1, 0) * qc, qc), qc), :] = jnp.zeros((qc, tb), BF16)
    sub = _iota((ch, tb), 0)
    for e, (c_lo, nch, qbase) in enumerate(chunks):
        parts = []
        for h in range(rpb):
            ce = code_ref[e:e + 1, h, 0, :]
            parts.append(jnp.where(ce >= 0, ce + shifts[e][h], -1))
        tgt = jnp.concatenate(parts, axis=1)

        def mark(i, carry, tgt=tgt, qbase=qbase):
            q0 = pl.multiple_of(qbase + i * ch, ch)
            sel_ref[pl.ds(q0, ch), :] = jnp.where(tgt == q0 + sub, 1.0, 0.0).astype(BF16)
            return carry

        lax.fori_loop(0, nch, mark, 0)

    def drain(i, carry):
        chunk_copy(0, 0, slot).wait()
        return carry

    lax.fori_loop(0, qtot // ch, drain, 0)

    def seg_sum(k, acc):
        q0 = pl.multiple_of(k * qc, qc)
        return acc + lax.dot_general(sel_ref[pl.ds(q0, qc), :], buf_ref[slot, pl.ds(q0, qc), :],
                                     (((0,), (0,)), ((), ())), preferred_element_type=F32)

    ffn = lax.fori_loop(0, nq, seg_sum, jnp.zeros(x1_ref.shape, F32))
    x2 = x1_ref[...] + ffn
    if final_norm:
        x2 = x2 * lax.rsqrt(jnp.mean(x2 * x2, axis=-1, keepdims=True) + EPS) * fw_ref[...]
    out_ref[...] = x2


def _combine(off, tot, code, x1, ye, final_w, cap, final_norm):
    t, d = x1.shape
    e = code.shape[0]
    rpb = _pick(t // LANES, (2, 1))
    tb = rpb * LANES
    qc = 256
    max_rows = e * (tb + 2 * BF16_ROWS)
    buf_rows = ((max_rows + qc - 1) // qc) * qc
    return pl.pallas_call(
        functools.partial(_combine_kernel, cap=cap, n_experts=e, qc=qc, final_norm=final_norm),
        grid_spec=pltpu.PrefetchScalarGridSpec(
            num_scalar_prefetch=2,
            grid=(t // tb,),
            in_specs=[
                pl.BlockSpec((e, rpb, 1, LANES), lambda i, *_: (0, i, 0, 0)),
                pl.BlockSpec((tb, d), lambda i, *_: (i, 0)),
                pl.BlockSpec((1, d), lambda i, *_: (0, 0)),
                pl.BlockSpec(memory_space=pl.ANY),
            ],
            out_specs=pl.BlockSpec((tb, d), lambda i, *_: (i, 0)),
            scratch_shapes=[pltpu.VMEM((2, buf_rows, d), ye.dtype), pltpu.VMEM((buf_rows, tb), BF16),
                            pltpu.SemaphoreType.DMA((2,))],
        ),
        out_shape=jax.ShapeDtypeStruct((t, d), F32),
        compiler_params=_cparams(("arbitrary",)),
        name="combine",
    )(off, tot, code.reshape(e, t // LANES, 1, LANES), x1, final_w, ye)


def _pick(n, prefs):
    for p in prefs:
        if n % p == 0:
            return p
    raise ValueError(f"no tile size for {n}")


def _layer_front(x, lw):
    batch, seq, d = x.shape
    t = batch * seq
    heads = lw["dt_bias_f"].shape[-1]
    d_inner = lw["ssm_norm_w"].shape[-1]
    conv_ch = lw["conv_w"].shape[-1]
    n_experts = lw["w_router"].shape[-1]
    cap = CAPACITY_FACTOR * t // n_experts
    assert seq % CHUNK == 0 and cap % LANES == 0 and 2 * heads <= LANES and CHUNK == LANES

    x2d = x.reshape(t, d)
    main, acum, src_t, wrow_t = _in_proj(
        x2d, lw["norm_mix_w"], lw["w_main"], lw["w_dt"], lw["dt_bias_all"], lw["a_log_all"], heads,
        d_inner, d_inner + conv_ch + d,
        _pick(t, (512, 256, 128)), _pick(lw["w_main"].shape[1], (1024, 512)))
    xc = _conv(main, lw["conv_w"], lw["conv_b"], seq, d_inner,
               _pick(seq, (1024, 512, 256, 128)), _pick(conv_ch, (1024, 512)))
    cpb = _pick(seq // CHUNK, (4, 2, 1))
    yf = _ssd(xc, acum, src_t, wrow_t, lw["d_skip_exp"], batch, seq, d_inner, heads, False, cpb)
    yb = _ssd(xc, acum, src_t, wrow_t, lw["d_skip_exp"], batch, seq, d_inner, heads, True, cpb)
    x1, h2, probs_t = _mix(yf, yb, main, x2d, lw["ssm_norm_w"], lw["w_ssm_out"], lw["w_pool"],
                           lw["pool_scale"], lw["w_o"], lw["norm_ffn_w"], lw["w_router_pad"],
                           seq, d_inner, conv_ch, n_experts, _pick(seq, (512, 256, 128)))
    idx, aff, code, off, tot = _route(probs_t, cap)
    xe = _gather(h2, idx.reshape(n_experts * cap))
    return dict(shape=x.shape, cap=cap, x1=x1, xe=xe, aff=aff, code=code, off=off, tot=tot)


def _layer_back(fr, lw, final_w, final_norm):
    batch, seq, d = fr["shape"]
    t = batch * seq
    cap = fr["cap"]
    n_experts = fr["code"].shape[0]
    ye = _ffn(fr["xe"], fr["aff"].reshape(n_experts * cap // LANES, LANES), lw["w_gate"], lw["w_up"],
              lw["w_down"], cap, _pick(cap, (512, 256, 128)), _pick(lw["w_gate"].shape[-1], (256, 128)))
    r = t // LANES
    out = _combine(fr["off"].reshape(n_experts, r), fr["tot"].reshape(n_experts, r), fr["code"], fr["x1"],
                   ye, final_w, cap, final_norm)
    return out.reshape(batch, seq, d)


def _prep_layer(i, norm_mix_w, w_in, conv_w, conv_b, dt_bias_f, dt_bias_b, a_log_f, a_log_b, d_skip,
                ssm_norm_w, w_ssm_out, w_pool, pool_scale, w_o, norm_ffn_w, w_router, w_gate, w_up, w_down):
    d = w_in.shape[1]
    heads = dt_bias_f.shape[-1]
    d_inner = ssm_norm_w.shape[-1]
    conv_ch = conv_w.shape[-1]
    p = d_inner // heads
    o2 = d_inner + conv_ch
    o4 = o2 + 2 * heads
    wi = w_in[i]
    w_main = jnp.concatenate([wi[:, :o2], wi[:, o4:]], axis=1).astype(BF16)
    w_dt = jnp.pad(wi[:, o2:o4], ((0, 0), (0, LANES - 2 * heads)))
    row = lambda v: v.reshape(1, -1).astype(F32)
    padl = lambda v: jnp.pad(v.reshape(1, -1).astype(F32), ((0, 0), (0, LANES - v.size)))
    return {
        "norm_mix_w": row(norm_mix_w[i]), "w_main": w_main, "w_dt": w_dt,
        "conv_w": conv_w[i], "conv_b": row(conv_b[i]),
        "dt_bias_f": dt_bias_f[i],
        "dt_bias_all": padl(jnp.concatenate([dt_bias_f[i], dt_bias_b[i]])),
        "a_log_all": padl(jnp.concatenate([a_log_f[i], a_log_b[i]])),
        "d_skip_exp": row(jnp.repeat(d_skip[i], p)),
        "ssm_norm_w": row(ssm_norm_w[i]), "w_ssm_out": w_ssm_out[i].astype(BF16),
        "w_pool": w_pool[i].astype(BF16), "pool_scale": row(pool_scale[i]),
        "w_o": w_o[i].astype(BF16), "norm_ffn_w": row(norm_ffn_w[i]),
        "w_router": w_router[i],
        "w_router_pad": jnp.pad(w_router[i].astype(F32), ((0, 0), (0, LANES - w_router.shape[-1]))),
        "w_gate": w_gate[i].astype(BF16), "w_up": w_up[i].astype(BF16), "w_down": w_down[i].astype(BF16),
    }


def kernel(x_prompt, x_sample, norm_mix_w, w_in, conv_w, conv_b, dt_bias_f, dt_bias_b, a_log_f, a_log_b,
           d_skip, ssm_norm_w, w_ssm_out, w_pool, pool_scale, w_o, norm_ffn_w, w_router, w_gate, w_up,
           w_down, norm_final_w):
    depth = w_in.shape[0]
    layers = [_prep_layer(i, norm_mix_w, w_in, conv_w, conv_b, dt_bias_f, dt_bias_b, a_log_f, a_log_b,
                          d_skip, ssm_norm_w, w_ssm_out, w_pool, pool_scale, w_o, norm_ffn_w, w_router,
                          w_gate, w_up, w_down) for i in range(depth)]
    final_w = norm_final_w.reshape(1, -1).astype(F32)

    xs = [x_prompt, x_sample]
    for i, lw in enumerate(layers):
        fronts = [_layer_front(x, lw) for x in xs]
        xs = [_layer_back(fr, lw, final_w, final_norm=(i == depth - 1)) for fr in fronts]
    return tuple(xs)
```

```python
import functools

import jax
import jax.numpy as jnp
from jax import lax
from jax.experimental import pallas as pl
from jax.experimental.pallas import tpu as pltpu
from jax.experimental.pallas import tpu_sc as plsc

F32 = jnp.float32
BF16 = jnp.bfloat16
I32 = jnp.int32

EPS = 1e-6
LOG2E = 1.4426950408889634
CHUNK = 128
LANES = 128
SUBLANES = 8
BF16_ROWS = 16
SSM_GROUPS = 8
POOL_WINDOWS = (2, 4, 8, 16)
CAPACITY_FACTOR = 2
VMEM_LIMIT = 56 * 1024 * 1024


def _cparams(sem):
    return pltpu.CompilerParams(dimension_semantics=sem, vmem_limit_bytes=VMEM_LIMIT)


def _iota(shape, dim):
    return lax.broadcasted_iota(I32, shape, dim)


def _sigmoid(x):
    return 1.0 / (1.0 + jnp.exp(-x))


def _split_bf16(x, terms):
    parts = []
    rem = x
    for _ in range(terms):
        p = rem.astype(BF16)
        parts.append(p)
        rem = rem - p.astype(F32)
    return parts


def _dot(a, b):
    return jnp.dot(a, b, preferred_element_type=F32)


def _dot_nt(a, b):
    return lax.dot_general(a, b, (((1,), (1,)), ((), ())), preferred_element_type=F32)


def _dot_exact_lhs(mask_bf, x, terms=3):
    out = None
    for p in _split_bf16(x, terms):
        d = _dot(mask_bf, p)
        out = d if out is None else out + d
    return out


def _in_proj_kernel(x_ref, nw_ref, w_ref, wdt_ref, bias_ref, alog_ref, main_ref, acum_ref, src_ref, wrow_ref,
                    h_ref, *, tn, heads, silu_end, sigmoid_start):
    x = x_ref[...]
    h = x * lax.rsqrt(jnp.mean(x * x, axis=-1, keepdims=True) + EPS) * nw_ref[...]
    hb, hl = _split_bf16(h, 2)
    h_ref[...] = hb
    wb, wl = _split_bf16(wdt_ref[...], 2)
    v = _dot(hb, wb) + _dot(hl, wb) + _dot(hb, wl) + bias_ref[...]
    dt = jnp.maximum(v, 0.0) + jnp.log1p(jnp.exp(-jnp.abs(v)))

    L = CHUNK
    neg_a = -jnp.exp(alog_ref[...])
    row = _iota((L, L), 0)
    col = _iota((L, L), 1)
    tril = jnp.where(col <= row, 1.0, 0.0).astype(BF16)
    triu = jnp.where(col >= row, 1.0, 0.0).astype(BF16)
    fwd_col = col < heads
    fwd_row = _iota((LANES, 1), 0) < heads
    for k in range(x_ref.shape[0] // L):
        rows = slice(k * L, (k + 1) * L)
        dtk = dt[rows, :]
        da = dtk * neg_a
        acum = jnp.where(fwd_col, _dot_exact_lhs(tril, da), _dot_exact_lhs(triu, da))
        acum_t = acum.T
        a_last_t = jnp.where(fwd_row, acum_t[:, L - 1:L], acum_t[:, 0:1])
        src_t = acum_t - jnp.log(dtk).T
        acum_ref[rows, :] = acum * LOG2E
        src_ref[rows, :] = src_t * LOG2E
        wrow_ref[rows, :] = jnp.exp(a_last_t - src_t)

    for j in range(w_ref.shape[1] // tn):
        o = _dot(h_ref[...], w_ref[:, j * tn:(j + 1) * tn])
        if (j + 1) * tn <= silu_end:
            o = o * _sigmoid(o)
        elif j * tn >= sigmoid_start:
            o = _sigmoid(o)
        else:
            assert j * tn >= silu_end and (j + 1) * tn <= sigmoid_start
        main_ref[:, j * tn:(j + 1) * tn] = o.astype(main_ref.dtype)


def _in_proj(x2d, norm_w, w_main, w_dt, dt_bias, alog_all, heads, silu_end, sigmoid_start, tm, tn):
    t, d = x2d.shape
    ncols = w_main.shape[1]
    const = lambda shape: pl.BlockSpec(shape, lambda i: (0, 0), pipeline_mode=pl.Buffered(1))
    table = pl.BlockSpec((tm, LANES), lambda i: (i, 0))
    return pl.pallas_call(
        functools.partial(_in_proj_kernel, tn=tn, heads=heads, silu_end=silu_end, sigmoid_start=sigmoid_start),
        grid=(t // tm,),
        in_specs=[
            pl.BlockSpec((tm, d), lambda i: (i, 0)),
            const((1, d)),
            const((d, ncols)),
            const((d, LANES)),
            const((1, LANES)),
            const((1, LANES)),
        ],
        out_specs=[pl.BlockSpec((tm, ncols), lambda i: (i, 0)), table, table, table],
        out_shape=[jax.ShapeDtypeStruct((t, ncols), BF16)] + [jax.ShapeDtypeStruct((t, LANES), F32)] * 3,
        scratch_shapes=[pltpu.VMEM((tm, d), BF16)],
        compiler_params=_cparams(("parallel",)),
        name="in_proj",
    )(x2d, norm_w, w_main, w_dt, dt_bias, alog_all)


def _conv_kernel(cur_ref, prev_ref, next_ref, w_ref, b_ref, out_ref, buf_ref, *, tiles_per_seq):
    i = pl.program_id(0)
    tr = cur_ref.shape[0]
    halo = prev_ref.shape[0]
    pos = i % tiles_per_seq
    keep_prev = jnp.where(pos == 0, 0.0, 1.0)
    keep_next = jnp.where(pos == tiles_per_seq - 1, 0.0, 1.0)
    buf_ref[0:halo, :] = (prev_ref[...].astype(F32) * keep_prev).astype(BF16)
    buf_ref[halo:halo + tr, :] = cur_ref[...]
    buf_ref[halo + tr:halo + tr + halo, :] = (next_ref[...].astype(F32) * keep_next).astype(BF16)
    k = w_ref.shape[0]
    pad = k // 2
    win = CHUNK + 2 * halo
    ti = _iota((CHUNK, win), 0)
    ji = _iota((CHUNK, win), 1)
    shift = {j: jnp.where(ji == ti + halo + (j - pad), 1.0, 0.0).astype(BF16) for j in range(k) if j != pad}
    for b in range(tr // CHUNK):
        xw = buf_ref[b * CHUNK:b * CHUNK + win, :]
        acc = b_ref[...] + w_ref[pad:pad + 1, :] * cur_ref[b * CHUNK:(b + 1) * CHUNK, :].astype(F32)
        for j in shift:
            acc = acc + w_ref[j:j + 1, :] * _dot(shift[j], xw)
        out_ref[b * CHUNK:(b + 1) * CHUNK, :] = (acc * _sigmoid(acc)).astype(out_ref.dtype)


def _conv(main, conv_w, conv_b, seq, col0, tr, tc):
    t = main.shape[0]
    k, ch = conv_w.shape
    halo = BF16_ROWS
    cb0 = col0 // tc
    nhalo = t // halo
    rh = tr // halo
    return pl.pallas_call(
        functools.partial(_conv_kernel, tiles_per_seq=seq // tr),
        grid=(t // tr, ch // tc),
        in_specs=[
            pl.BlockSpec((tr, tc), lambda i, j: (i, cb0 + j)),
            pl.BlockSpec((halo, tc), lambda i, j: (jnp.maximum(i * rh - 1, 0), cb0 + j)),
            pl.BlockSpec((halo, tc), lambda i, j: (jnp.minimum((i + 1) * rh, nhalo - 1), cb0 + j)),
            pl.BlockSpec((k, tc), lambda i, j: (0, j)),
            pl.BlockSpec((1, tc), lambda i, j: (0, j)),
        ],
        out_specs=pl.BlockSpec((tr, tc), lambda i, j: (i, j)),
        out_shape=jax.ShapeDtypeStruct((t, ch), BF16),
        scratch_shapes=[pltpu.VMEM((tr + 2 * halo, tc), BF16)],
        compiler_params=_cparams(("parallel", "parallel")),
        name="conv_silu",
    )(main, main, main, conv_w, conv_b)


def _ssd_kernel(xs_ref, b_ref, c_ref, acum_ref, src_ref, wrow_ref, dskip_ref, y_ref, state_ref,
                *, reverse, heads, add_skip):
    @pl.when(pl.program_id(1) == 0)
    def _():
        state_ref[...] = jnp.zeros_like(state_ref)

    L = CHUNK
    cpb = xs_ref.shape[0] // L
    d_inner = xs_ref.shape[1]
    p = d_inner // heads
    hpg = heads // SSM_GROUPS
    n = b_ref.shape[1] // SSM_GROUPS
    assert n == L
    assert hpg % 2 == 0
    off = heads if reverse else 0
    first = _iota((L, 2 * p), 1) < p

    row = _iota((L, L), 0)
    col = _iota((L, L), 1)
    mask = (col >= row) if reverse else (col <= row)
    last = 0 if reverse else L - 1

    def chunk(k, carry):
        r0 = pl.multiple_of(((cpb - 1 - k) if reverse else k) * L, L)
        rows = pl.ds(r0, L)
        acum = acum_ref[rows, :]
        src_t = src_ref[rows, :]
        wrow_t = wrow_ref[rows, :]
        dchunk = jnp.exp2(acum[last:last + 1, :])

        for g in range(SSM_GROUPS):
            bg = b_ref[rows, g * n:(g + 1) * n]
            cg = c_ref[rows, g * n:(g + 1) * n]
            cgf = cg.astype(F32)
            cb = _dot_nt(cg, bg)
            btf = bg.astype(F32).T
            for j in range(0, hpg, 2):
                hh0 = g * hpg + j
                cols = slice(hh0 * p, (hh0 + 2) * p)
                xpair = xs_ref[rows, cols]
                st = state_ref[hh0 // 2]
                rhs = jnp.concatenate([xpair, st.astype(BF16)], axis=0)
                ys, ds, dch = [], [], []
                for hh in (hh0, hh0 + 1):
                    h = off + hh
                    colb = jnp.broadcast_to(acum[:, h:h + 1], (L, L))
                    m = (cb * jnp.exp2(jnp.where(mask, colb - src_t[h:h + 1, :], -1e30))).astype(BF16)
                    cs = (cgf * jnp.exp2(colb)).astype(BF16)
                    ys.append(_dot(jnp.concatenate([m, cs], axis=1), rhs))
                    bts = (btf * wrow_t[h:h + 1, :]).astype(BF16)
                    ds.append(_dot(bts, xpair))
                    dch.append(dchunk[:, h:h + 1])
                y = jnp.where(first, ys[0], ys[1])
                state_ref[hh0 // 2] = (st * jnp.where(first[0:1, :], dch[0], dch[1])
                                       + jnp.where(first, ds[0], ds[1]))
                if add_skip:
                    y = y + xpair.astype(F32) * dskip_ref[:, cols]
                y_ref[rows, cols] = y.astype(y_ref.dtype)
        return carry

    lax.fori_loop(0, cpb, chunk, 0)


def _ssd(xc, acum, src_t, wrow_t, dskip_exp, batch, seq, d_inner, heads, reverse, cpb):
    t = xc.shape[0]
    nb = seq // (cpb * CHUNK)
    rb = cpb * CHUNK
    gn = (xc.shape[1] - d_inner) // 2
    bblk = d_inner // gn
    n = gn // SSM_GROUPS

    def rowblk(b, c):
        return b * nb + (nb - 1 - c if reverse else c)

    return pl.pallas_call(
        functools.partial(_ssd_kernel, reverse=reverse, heads=heads, add_skip=not reverse),
        grid=(batch, nb),
        in_specs=[
            pl.BlockSpec((rb, d_inner), lambda b, c: (rowblk(b, c), 0)),
            pl.BlockSpec((rb, gn), lambda b, c: (rowblk(b, c), bblk)),
            pl.BlockSpec((rb, gn), lambda b, c: (rowblk(b, c), bblk + 1)),
            pl.BlockSpec((rb, LANES), lambda b, c: (rowblk(b, c), 0)),
            pl.BlockSpec((rb, LANES), lambda b, c: (rowblk(b, c), 0)),
            pl.BlockSpec((rb, LANES), lambda b, c: (rowblk(b, c), 0)),
            pl.BlockSpec((1, d_inner), lambda b, c: (0, 0)),
        ],
        out_specs=pl.BlockSpec((rb, d_inner), lambda b, c: (rowblk(b, c), 0)),
        out_shape=jax.ShapeDtypeStruct((t, d_inner), BF16),
        scratch_shapes=[pltpu.VMEM((heads // 2, n, 2 * d_inner // heads), F32)],
        compiler_params=_cparams(("parallel", "arbitrary")),
        name="ssd_bwd" if reverse else "ssd_fwd",
    )(xc, xc, xc, acum, src_t, wrow_t, dskip_exp)


def _mix_kernel(yf_ref, yb_ref, z_ref, u_ref, up_ref, un_ref, ga_ref, gb_ref, x_ref,
                snw_ref, wso_ref, wp_ref, ps_ref, wo_ref, fnw_ref, wr_ref,
                x1_ref, h2_ref, pt_ref, *, tiles_per_seq, seq, n_experts):
    tm = x_ref.shape[0]
    d_inner = z_ref.shape[1]
    gw = d_inner // SSM_GROUPS
    halo = up_ref.shape[0]
    sub = min(tm, 2 * CHUNK)
    pos0 = (pl.program_id(0) % tiles_per_seq) * tm
    pw = u_ref.shape[1] // len(POOL_WINDOWS)
    keep_prev = jnp.where(pos0 > 0, 1.0, 0.0)
    keep_next = jnp.where(pos0 + tm < seq, 1.0, 0.0)
    wb, wl = _split_bf16(wr_ref[...], 2)

    tq = _iota((sub, sub), 0)
    sq = _iota((sub, sub), 1)
    tp = _iota((sub, halo), 0)
    jp = _iota((sub, halo), 1)
    bands = []
    for win in POOL_WINDOWS:
        lo = win // 2
        hi = win - lo
        in_c = (sq >= tq - lo) & (sq < tq + hi)
        in_p = jp - halo >= tp - lo
        in_n = jp + sub < tp + hi
        bands.append((jnp.where(in_c, 1.0, 0.0).astype(BF16), in_p, in_n))

    for si in range(tm // sub):
        rows = slice(si * sub, (si + 1) * sub)
        at_top = si == 0
        at_bottom = si == tm // sub - 1

        y = (yf_ref[rows, :].astype(F32) + yb_ref[rows, :].astype(F32)) * z_ref[rows, :].astype(F32)
        parts = []
        for g in range(SSM_GROUPS):
            blk = y[:, g * gw:(g + 1) * gw]
            sc = lax.rsqrt(jnp.mean(blk * blk, axis=-1, keepdims=True) + EPS)
            parts.append((blk * sc * snw_ref[:, g * gw:(g + 1) * gw]).astype(BF16))
        y_a = _dot(jnp.concatenate(parts, axis=1), wso_ref[...])

        tpos = pos0 + si * sub + _iota((sub, 1), 0)
        pooled = []
        for gi, win in enumerate(POOL_WINDOWS):
            lo = win // 2
            hi = win - lo
            band_c, in_p, in_n = bands[gi]
            cg = slice(gi * pw, (gi + 1) * pw)
            ug = u_ref[rows, cg]
            u_prev = up_ref[:, cg] if at_top else u_ref[si * sub - halo:si * sub, cg]
            u_next = un_ref[:, cg] if at_bottom else u_ref[(si + 1) * sub:(si + 1) * sub + halo, cg]
            band_p = jnp.where(in_p, keep_prev if at_top else 1.0, 0.0).astype(BF16)
            band_n = jnp.where(in_n, keep_next if at_bottom else 1.0, 0.0).astype(BF16)
            s = _dot(band_c, ug) + _dot(band_p, u_prev) + _dot(band_n, u_next)
            cnt = (jnp.minimum(tpos + hi, seq) - jnp.maximum(tpos - lo, 0)).astype(F32)
            pg = (s / cnt - ug.astype(F32)).astype(BF16)
            pooled.append(_dot(pg, wp_ref[gi]))
        y_b = jnp.concatenate(pooled, axis=1) * ps_ref[...]

        merged = ga_ref[rows, :].astype(F32) * y_a + gb_ref[rows, :].astype(F32) * y_b
        x1 = x_ref[rows, :] + _dot(merged.astype(BF16), wo_ref[...])
        x1_ref[rows, :] = x1

        h2 = x1 * lax.rsqrt(jnp.mean(x1 * x1, axis=-1, keepdims=True) + EPS) * fnw_ref[...]
        h2_ref[rows, :] = h2
        hb, hl = _split_bf16(h2, 2)
        logits = _dot(hb, wb) + _dot(hl, wb) + _dot(hb, wl)
        logits = jnp.where(_iota(logits.shape, 1) < n_experts, logits, -1e30)
        e = jnp.exp(logits - jnp.max(logits, axis=-1, keepdims=True))
        probs = e / jnp.sum(e, axis=-1, keepdims=True)
        pt_ref[:, rows] = probs.T[:n_experts, :]


def _mix(yf, yb, main, x2d, ssm_norm_w, w_ssm_out, w_pool, pool_scale, w_o, norm_ffn_w, w_router_pad,
         seq, d_inner, conv_ch, n_experts, tm):
    t, d = x2d.shape
    halo = BF16_ROWS
    rh = tm // halo
    nhalo = t // halo
    zb = 0
    ub = (d_inner + conv_ch) // d
    full = lambda shape: pl.BlockSpec(shape, lambda i: (0,) * len(shape))
    return pl.pallas_call(
        functools.partial(_mix_kernel, tiles_per_seq=seq // tm, seq=seq, n_experts=n_experts),
        grid=(t // tm,),
        in_specs=[
            pl.BlockSpec((tm, d_inner), lambda i: (i, 0)),
            pl.BlockSpec((tm, d_inner), lambda i: (i, 0)),
            pl.BlockSpec((tm, d_inner), lambda i: (i, zb)),
            pl.BlockSpec((tm, d), lambda i: (i, ub)),
            pl.BlockSpec((halo, d), lambda i: (jnp.maximum(i * rh - 1, 0), ub)),
            pl.BlockSpec((halo, d), lambda i: (jnp.minimum((i + 1) * rh, nhalo - 1), ub)),
            pl.BlockSpec((tm, d), lambda i: (i, ub + 1)),
            pl.BlockSpec((tm, d), lambda i: (i, ub + 2)),
            pl.BlockSpec((tm, d), lambda i: (i, 0)),
            full((1, d_inner)),
            full((d_inner, d)),
            full(w_pool.shape),
            full((1, d)),
            full((d, d)),
            full((1, d)),
            full((d, LANES)),
        ],
        out_specs=[
            pl.BlockSpec((tm, d), lambda i: (i, 0)),
            pl.BlockSpec((tm, d), lambda i: (i, 0)),
            pl.BlockSpec((n_experts, tm), lambda i: (0, i)),
        ],
        out_shape=[
            jax.ShapeDtypeStruct((t, d), F32),
            jax.ShapeDtypeStruct((t, d), F32),
            jax.ShapeDtypeStruct((n_experts, t), F32),
        ],
        compiler_params=_cparams(("parallel",)),
        name="mix",
    )(yf, yb, main, main, main, main, main, main, x2d,
      ssm_norm_w, w_ssm_out, w_pool, pool_scale, w_o, norm_ffn_w, w_router_pad)


def _threshold_kernel(p_ref, tau_ref, *, cap):
    key = pltpu.bitcast(p_ref[...], I32)

    def search(i, prefix):
        cand = prefix | lax.shift_left(jnp.int32(1), 30 - i)
        cnt = jnp.sum(jnp.where(key >= cand, 1.0, 0.0), axis=(1, 2), keepdims=True)
        return jnp.where(cnt >= cap, cand, prefix)

    tau = lax.fori_loop(0, 31, search, jnp.zeros((key.shape[0], 1, 1), I32))
    tau_ref[...] = jnp.broadcast_to(tau, tau_ref.shape)


def _threshold(p3, cap):
    e, r, _ = p3.shape
    return pl.pallas_call(
        functools.partial(_threshold_kernel, cap=cap),
        grid=(1,),
        in_specs=[pl.BlockSpec((e, r, LANES), lambda i: (0, 0, 0))],
        out_specs=pl.BlockSpec((e, 1, LANES), lambda i: (0, 0, 0)),
        out_shape=jax.ShapeDtypeStruct((e, 1, LANES), I32),
        compiler_params=_cparams(("arbitrary",)),
        name="route_threshold",
    )(p3)


def _route_kernel(p_ref, tau_ref, idx_ref, aff_ref, code_ref, off_ref, tot_ref, *, cap, ct):
    p = p_ref[0]
    r = p.shape[0]
    key = pltpu.bitcast(p, I32)
    tau = tau_ref[0]

    lane_r = _iota((LANES, LANES), 0)
    lane_c = _iota((LANES, LANES), 1)
    tri_incl = jnp.where(lane_r <= lane_c, 1.0, 0.0).astype(BF16)
    rr = _iota((r, r), 0)
    rc = _iota((r, r), 1)
    tril_strict = jnp.where(rc < rr, 1.0, 0.0).astype(BF16)

    def cumsum(mask):
        mb = jnp.where(mask, 1.0, 0.0).astype(BF16)
        incl = _dot(mb, tri_incl)
        tot = jnp.broadcast_to(incl[:, LANES - 1:LANES], (r, LANES))
        off = _dot(tril_strict, tot.astype(BF16))
        return mb, incl, off, tot

    gt = key > tau
    eq = key == tau
    need = (cap - jnp.sum(jnp.where(gt, 1, 0))).astype(F32)
    mb_eq, incl_eq, off_eq, _ = cumsum(eq)
    rank_eq = off_eq + incl_eq - mb_eq.astype(F32)
    sel = gt | (eq & (rank_eq < need))
    mb, incl, off, tot = cumsum(sel)

    local = incl - mb.astype(F32)
    code_ref[0] = jnp.where(sel, local, -1.0).astype(I32)

    ones8 = jnp.ones((SUBLANES, LANES), BF16)
    tot_t = _dot_nt(ones8, mb)
    triu_strict = jnp.where(rr < rc, 1.0, 0.0).astype(BF16)
    off_t = _dot(tot_t.astype(BF16), triu_strict)
    off_ref[0] = off_t[0:1, :].astype(I32)
    tot_ref[0] = tot_t[0:1, :].astype(I32)

    incl_t = _dot_nt(jnp.where(lane_c <= lane_r, 1.0, 0.0).astype(BF16), mb).astype(BF16)
    p_t = _split_bf16(p.T, 3)
    incl_row = (off + tot)[:, 0:1]
    off_col = off[:, 0:1]
    for c0 in range(0, cap, ct):
        cvec = (c0 + _iota((1, ct), 1)).astype(F32)
        before = incl_row <= cvec
        rsel = jnp.sum(jnp.where(before, 1, 0), axis=0, keepdims=True)
        onehot = _iota((r, ct), 0) == rsel
        onehot_bf = jnp.where(onehot, 1.0, 0.0).astype(BF16)
        g_t = _dot(incl_t, onehot_bf)
        offc = jnp.sum(jnp.where(onehot, off_col, 0.0), axis=0, keepdims=True)
        target = cvec - offc
        lane = jnp.sum(jnp.where(g_t <= target, 1, 0), axis=0, keepdims=True)
        idx_ref[0, :, c0:c0 + ct] = rsel * LANES + lane
        a_t = _dot(p_t[0], onehot_bf) + _dot(p_t[1], onehot_bf) + _dot(p_t[2], onehot_bf)
        hit = _iota((LANES, ct), 0) == lane
        aff_ref[0, :, c0:c0 + ct] = jnp.sum(jnp.where(hit, a_t, 0.0), axis=0, keepdims=True)


def _route(probs_t, cap):
    e, t = probs_t.shape
    r = t // LANES
    ct = min(cap, 512)
    p3 = probs_t.reshape(e, r, LANES)
    return pl.pallas_call(
        functools.partial(_route_kernel, cap=cap, ct=ct),
        grid=(e,),
        in_specs=[pl.BlockSpec((1, r, LANES), lambda i: (i, 0, 0)),
                  pl.BlockSpec((1, 1, LANES), lambda i: (i, 0, 0))],
        out_specs=[
            pl.BlockSpec((1, 1, cap), lambda i: (i, 0, 0)),
            pl.BlockSpec((1, 1, cap), lambda i: (i, 0, 0)),
            pl.BlockSpec((1, r, LANES), lambda i: (i, 0, 0)),
            pl.BlockSpec((1, 1, r), lambda i: (i, 0, 0)),
            pl.BlockSpec((1, 1, r), lambda i: (i, 0, 0)),
        ],
        out_shape=[
            jax.ShapeDtypeStruct((e, 1, cap), I32),
            jax.ShapeDtypeStruct((e, 1, cap), F32),
            jax.ShapeDtypeStruct((e, r, LANES), I32),
            jax.ShapeDtypeStruct((e, 1, r), I32),
            jax.ShapeDtypeStruct((e, 1, r), I32),
        ],
        compiler_params=_cparams(("parallel",)),
        name="route",
    )(p3, _threshold(p3, cap))


SC_CORES = 2
SC_SUBCORES = 16
SC_GATHER_ROWS = 32


def _gather(h2, idx_flat):
    t, d = h2.shape
    nrows = idx_flat.shape[0]
    nc, ns = SC_CORES, SC_SUBCORES
    nw = nc * ns
    bw = SC_GATHER_ROWS
    per_w = nrows // nw
    nb = per_w // bw
    assert nrows % (nw * bw * 2) == 0
    mesh = plsc.VectorSubcoreMesh(core_axis_name="c", subcore_axis_name="s", num_cores=nc, num_subcores=ns)

    @functools.partial(
        pl.kernel, mesh=mesh, out_type=jax.ShapeDtypeStruct((nrows, d), h2.dtype),
        scratch_types=[pltpu.VMEM((nb, bw), I32), pltpu.VMEM((bw, d), h2.dtype), pltpu.VMEM((bw, d), h2.dtype),
                       pltpu.SemaphoreType.DMA, pltpu.SemaphoreType.DMA],
        name="gather")
    def run(table_hbm, idx_hbm, out_hbm, idx_v, rows_a, rows_b, sem_a, sem_b):
        wid = lax.axis_index("s") * nc + lax.axis_index("c")
        base = wid * per_w
        pltpu.sync_copy(idx_hbm.at[wid], idx_v)

        def gather(j, buf, sem):
            return pltpu.make_async_copy(table_hbm.at[idx_v.at[j]], buf, sem)

        gather(0, rows_a, sem_a).start()

        @pl.loop(0, nb, step=2)
        def _(j):
            gather(j + 1, rows_b, sem_b).start()
            gather(j, rows_a, sem_a).wait()
            pltpu.sync_copy(rows_a, out_hbm.at[pl.ds(base + j * bw, bw)])

            @pl.when(j + 2 < nb)
            def _():
                gather(j + 2, rows_a, sem_a).start()

            gather(j + 1, rows_b, sem_b).wait()
            pltpu.sync_copy(rows_b, out_hbm.at[pl.ds(base + (j + 1) * bw, bw)])

    return run(h2, idx_flat.reshape(nw, nb, bw))


def _ffn_kernel(xe_ref, aff_ref, wg_ref, wu_ref, wd_ref, out_ref, xb_ref, hid_ref, *, fc):
    xb_ref[...] = xe_ref[...].astype(BF16)
    for c in range(wg_ref.shape[1] // fc):
        g = _dot(xb_ref[...], wg_ref[:, c * fc:(c + 1) * fc])
        u = _dot(xb_ref[...], wu_ref[:, c * fc:(c + 1) * fc])
        hid_ref[:, c * fc:(c + 1) * fc] = (g * _sigmoid(g) * u).astype(BF16)
    y = _dot(hid_ref[...], wd_ref[...])
    for j in range(aff_ref.shape[0]):
        a = jnp.broadcast_to(aff_ref[j:j + 1, :], (LANES, LANES)).T[:, 0:1]
        out_ref[j * LANES:(j + 1) * LANES, :] = (y[j * LANES:(j + 1) * LANES, :] * a).astype(out_ref.dtype)


def _ffn(xe, aff2d, w_gate, w_up, w_down, cap, rs, fc):
    nrows, d = xe.shape
    e, _, ff = w_gate.shape
    rpe = cap // rs
    return pl.pallas_call(
        functools.partial(_ffn_kernel, fc=fc),
        grid=(e, rpe),
        in_specs=[
            pl.BlockSpec((rs, d), lambda ei, r: (ei * rpe + r, 0)),
            pl.BlockSpec((None, rs // LANES, LANES), lambda ei, r: (ei * rpe + r, 0, 0)),
            pl.BlockSpec((None, d, ff), lambda ei, r: (ei, 0, 0)),
            pl.BlockSpec((None, d, ff), lambda ei, r: (ei, 0, 0)),
            pl.BlockSpec((None, ff, d), lambda ei, r: (ei, 0, 0)),
        ],
        out_specs=pl.BlockSpec((rs, d), lambda ei, r: (ei * rpe + r, 0)),
        out_shape=jax.ShapeDtypeStruct((nrows, d), BF16),
        scratch_shapes=[pltpu.VMEM((rs, d), BF16), pltpu.VMEM((rs, ff), BF16)],
        compiler_params=_cparams(("parallel", "arbitrary")),
        name="expert_ffn",
    )(xe, aff2d.reshape(nrows // rs, rs // LANES, LANES), w_gate, w_up, w_down)


def _combine_kernel(off_ref, tot_ref, code_ref, x1_ref, fw_ref, ye_hbm, out_ref, buf_ref, sel_ref, sem,
                    *, cap, n_experts, qc, final_norm):
    blk = pl.program_id(0)
    nblk = pl.num_programs(0)
    tb = x1_ref.shape[0]

    ch = BF16_ROWS

    rpb = code_ref.shape[1]

    def layout(b):
        qbase = jnp.int32(0)
        chunks, shifts = [], []
        for e in range(n_experts):
            s = off_ref[e, b * rpb]
            c_lo = s // ch
            first_row = qbase + (s - c_lo * ch)
            n = jnp.int32(0)
            per_row = []
            for h in range(rpb):
                per_row.append(first_row + n)
                n = n + tot_ref[e, b * rpb + h]
            nch = jnp.where(n > 0, (s + n - 1) // ch - c_lo + 1, 0)
            chunks.append((c_lo, nch, qbase))
            shifts.append(per_row)
            qbase = qbase + nch * ch
        return chunks, shifts, qbase

    def chunk_copy(src, dst, slot):
        return pltpu.make_async_copy(ye_hbm.at[pl.ds(src, ch), :],
                                     buf_ref.at[slot, pl.ds(dst, ch), :], sem.at[slot])

    def fetch(b, slot):
        chunks, _, _ = layout(b)
        for e, (c_lo, nch, qbase) in enumerate(chunks):
            def issue(i, carry, e=e, c_lo=c_lo, qbase=qbase):
                src = pl.multiple_of(e * cap + (c_lo + i) * ch, ch)
                dst = pl.multiple_of(qbase + i * ch, ch)
                chunk_copy(src, dst, slot).start()
                return carry

            lax.fori_loop(0, nch, issue, 0)

    slot = blk % 2

    @pl.when(blk == 0)
    def _():
        buf_ref[...] = jnp.zeros_like(buf_ref)
        fetch(blk, slot)

    @pl.when(blk + 1 < nblk)
    def _():
        fetch(blk + 1, 1 - slot)

    chunks, shifts, qtot = layout(blk)
    nq = (qtot + qc - 1) // qc

    sel_ref[pl.ds(pl.multiple_of(jnp.maximum(nq - 1, 0) * qc, qc), qc), :] = jnp.zeros((qc, tb), BF16)
    sub = _iota((ch, tb), 0)
    for e, (c_lo, nch, qbase) in enumerate(chunks):
        parts = []
        for h in range(rpb):
            ce = code_ref[e:e + 1, h, 0, :]
            parts.append(jnp.where(ce >= 0, ce + shifts[e][h], -1))
        tgt = jnp.concatenate(parts, axis=1)

        def mark(i, carry, tgt=tgt, qbase=qbase):
            q0 = pl.multiple_of(qbase + i * ch, ch)
            sel_ref[pl.ds(q0, ch), :] = jnp.where(tgt == q0 + sub, 1.0, 0.0).astype(BF16)
            return carry

        lax.fori_loop(0, nch, mark, 0)

    def drain(i, carry):
        chunk_copy(0, 0, slot).wait()
        return carry

    lax.fori_loop(0, qtot // ch, drain, 0)

    def seg_sum(k, acc):
        q0 = pl.multiple_of(k * qc, qc)
        return acc + lax.dot_general(sel_ref[pl.ds(q0, qc), :], buf_ref[slot, pl.ds(q0, qc), :],
                                     (((0,), (0,)), ((), ())), preferred_element_type=F32)

    ffn = lax.fori_loop(0, nq, seg_sum, jnp.zeros(x1_ref.shape, F32))
    x2 = x1_ref[...] + ffn
    if final_norm:
        x2 = x2 * lax.rsqrt(jnp.mean(x2 * x2, axis=-1, keepdims=True) + EPS) * fw_ref[...]
    out_ref[...] = x2


def _combine(off, tot, code, x1, ye, final_w, cap, final_norm):
    t, d = x1.shape
    e = code.shape[0]
    rpb = _pick(t // LANES, (2, 1))
    tb = rpb * LANES
    qc = 256
    max_rows = e * (tb + 2 * BF16_ROWS)
    buf_rows = ((max_rows + qc - 1) // qc) * qc
    return pl.pallas_call(
        functools.partial(_combine_kernel, cap=cap, n_experts=e, qc=qc, final_norm=final_norm),
        grid_spec=pltpu.PrefetchScalarGridSpec(
            num_scalar_prefetch=2,
            grid=(t // tb,),
            in_specs=[
                pl.BlockSpec((e, rpb, 1, LANES), lambda i, *_: (0, i, 0, 0)),
                pl.BlockSpec((tb, d), lambda i, *_: (i, 0)),
                pl.BlockSpec((1, d), lambda i, *_: (0, 0)),
                pl.BlockSpec(memory_space=pl.ANY),
            ],
            out_specs=pl.BlockSpec((tb, d), lambda i, *_: (i, 0)),
            scratch_shapes=[pltpu.VMEM((2, buf_rows, d), ye.dtype), pltpu.VMEM((buf_rows, tb), BF16),
                            pltpu.SemaphoreType.DMA((2,))],
        ),
        out_shape=jax.ShapeDtypeStruct((t, d), F32),
        compiler_params=_cparams(("arbitrary",)),
        name="combine",
    )(off, tot, code.reshape(e, t // LANES, 1, LANES), x1, final_w, ye)


def _pick(n, prefs):
    for p in prefs:
        if n % p == 0:
            return p
    raise ValueError(f"no tile size for {n}")


def _layer_front(x, lw):
    batch, seq, d = x.shape
    t = batch * seq
    heads = lw["dt_bias_f"].shape[-1]
    d_inner = lw["ssm_norm_w"].shape[-1]
    conv_ch = lw["conv_w"].shape[-1]
    n_experts = lw["w_router"].shape[-1]
    cap = CAPACITY_FACTOR * t // n_experts
    assert seq % CHUNK == 0 and cap % LANES == 0 and 2 * heads <= LANES and CHUNK == LANES

    x2d = x.reshape(t, d)
    main, acum, src_t, wrow_t = _in_proj(
        x2d, lw["norm_mix_w"], lw["w_main"], lw["w_dt"], lw["dt_bias_all"], lw["a_log_all"], heads,
        d_inner, d_inner + conv_ch + d,
        _pick(t, (512, 256, 128)), _pick(lw["w_main"].shape[1], (1024, 512)))
    xc = _conv(main, lw["conv_w"], lw["conv_b"], seq, d_inner,
               _pick(seq, (1024, 512, 256, 128)), _pick(conv_ch, (1024, 512)))
    cpb = _pick(seq // CHUNK, (4, 2, 1))
    yf = _ssd(xc, acum, src_t, wrow_t, lw["d_skip_exp"], batch, seq, d_inner, heads, False, cpb)
    yb = _ssd(xc, acum, src_t, wrow_t, lw["d_skip_exp"], batch, seq, d_inner, heads, True, cpb)
    x1, h2, probs_t = _mix(yf, yb, main, x2d, lw["ssm_norm_w"], lw["w_ssm_out"], lw["w_pool"],
                           lw["pool_scale"], lw["w_o"], lw["norm_ffn_w"], lw["w_router_pad"],
                           seq, d_inner, conv_ch, n_experts, _pick(seq, (512, 256, 128)))
    idx, aff, code, off, tot = _route(probs_t, cap)
    xe = _gather(h2, idx.reshape(n_experts * cap))
    return dict(shape=x.shape, cap=cap, x1=x1, xe=xe, aff=aff, code=code, off=off, tot=tot)


def _layer_back(fr, lw, final_w, final_norm):
    batch, seq, d = fr["shape"]
    t = batch * seq
    cap = fr["cap"]
    n_experts = fr["code"].shape[0]
    ye = _ffn(fr["xe"], fr["aff"].reshape(n_experts * cap // LANES, LANES), lw["w_gate"], lw["w_up"],
              lw["w_down"], cap, _pick(cap, (512, 256, 128)), _pick(lw["w_gate"].shape[-1], (256, 128)))
    r = t // LANES
    out = _combine(fr["off"].reshape(n_experts, r), fr["tot"].reshape(n_experts, r), fr["code"], fr["x1"],
                   ye, final_w, cap, final_norm)
    return out.reshape(batch, seq, d)


def _prep_layer(i, norm_mix_w, w_in, conv_w, conv_b, dt_bias_f, dt_bias_b, a_log_f, a_log_b, d_skip,
                ssm_norm_w, w_ssm_out, w_pool, pool_scale, w_o, norm_ffn_w, w_router, w_gate, w_up, w_down):
    d = w_in.shape[1]
    heads = dt_bias_f.shape[-1]
    d_inner = ssm_norm_w.shape[-1]
    conv_ch = conv_w.shape[-1]
    p = d_inner // heads
    o2 = d_inner + conv_ch
    o4 = o2 + 2 * heads
    wi = w_in[i]
    w_main = jnp.concatenate([wi[:, :o2], wi[:, o4:]], axis=1).astype(BF16)
    w_dt = jnp.pad(wi[:, o2:o4], ((0, 0), (0, LANES - 2 * heads)))
    row = lambda v: v.reshape(1, -1).astype(F32)
    padl = lambda v: jnp.pad(v.reshape(1, -1).astype(F32), ((0, 0), (0, LANES - v.size)))
    return {
        "norm_mix_w": row(norm_mix_w[i]), "w_main": w_main, "w_dt": w_dt,
        "conv_w": conv_w[i], "conv_b": row(conv_b[i]),
        "dt_bias_f": dt_bias_f[i],
        "dt_bias_all": padl(jnp.concatenate([dt_bias_f[i], dt_bias_b[i]])),
        "a_log_all": padl(jnp.concatenate([a_log_f[i], a_log_b[i]])),
        "d_skip_exp": row(jnp.repeat(d_skip[i], p)),
        "ssm_norm_w": row(ssm_norm_w[i]), "w_ssm_out": w_ssm_out[i].astype(BF16),
        "w_pool": w_pool[i].astype(BF16), "pool_scale": row(pool_scale[i]),
        "w_o": w_o[i].astype(BF16), "norm_ffn_w": row(norm_ffn_w[i]),
        "w_router": w_router[i],
        "w_router_pad": jnp.pad(w_router[i].astype(F32), ((0, 0), (0, LANES - w_router.shape[-1]))),
        "w_gate": w_gate[i].astype(BF16), "w_up": w_up[i].astype(BF16), "w_down": w_down[i].astype(BF16),
    }


def kernel(x_prompt, x_sample, norm_mix_w, w_in, conv_w, conv_b, dt_bias_f, dt_bias_b, a_log_f, a_log_b,
           d_skip, ssm_norm_w, w_ssm_out, w_pool, pool_scale, w_o, norm_ffn_w, w_router, w_gate, w_up,
           w_down, norm_final_w):
    depth = w_in.shape[0]
    layers = [_prep_layer(i, norm_mix_w, w_in, conv_w, conv_b, dt_bias_f, dt_bias_b, a_log_f, a_log_b,
                          d_skip, ssm_norm_w, w_ssm_out, w_pool, pool_scale, w_o, norm_ffn_w, w_router,
                          w_gate, w_up, w_down) for i in range(depth)]
    final_w = norm_final_w.reshape(1, -1).astype(F32)

    xs = [x_prompt, x_sample]
    for i, lw in enumerate(layers):
        fronts = [_layer_front(x, lw) for x in xs]
        xs = [_layer_back(fr, lw, final_w, final_norm=(i == depth - 1)) for fr in fronts]
    return tuple(xs)
```
